```python
import math
import jax, jax.numpy as jnp
from jax import lax
import numpy as np

D_MODEL = 1024
BATCH = 8
SEQ = 2048
DEPTH = 4
DEC_BATCH = 128
DEC_SEQ = 1
PAST_LEN = 16384
PAGE_SIZE = 128

CONF_CH = D_MODEL // 2
CONF_KW = 31
N_HEADS = 4
HEAD_K = 128
HEAD_V = 128
QK_DIM = N_HEADS * HEAD_K
V_DIM = N_HEADS * HEAD_V
QKV_DIM = 2 * QK_DIM + V_DIM
SHORT_KW = 4
CHUNK = 64
D_FF = -(-(8 * D_MODEL) // (3 * 256)) * 256
RMS_EPS = 1e-6
LN_EPS = 1e-5
O_GLU_A = CONF_CH
O_GLU_B = 2 * CONF_CH
O_QKV = O_GLU_B + QKV_DIM
O_Z = O_QKV + V_DIM
O_BETA = O_Z + N_HEADS
O_ALPHA = O_BETA + N_HEADS
O_MERGE_A = O_ALPHA + D_MODEL
IN_DIM = O_MERGE_A + D_MODEL

kernel_name = "hybrid_conformer_gdn_adaln_step"


def rms_norm(x, g, eps=RMS_EPS):
    xf = x.astype(jnp.float32)
    y = xf * lax.rsqrt(jnp.mean(xf * xf, axis=-1, keepdims=True) + eps)
    return (y * g.astype(jnp.float32)).astype(x.dtype)


def layer_norm(x, g, b, eps=LN_EPS):
    xf = x.astype(jnp.float32)
    mu = jnp.mean(xf, axis=-1, keepdims=True)
    var = jnp.mean(jnp.square(xf - mu), axis=-1, keepdims=True)
    y = (xf - mu) * lax.rsqrt(var + eps)
    return (y * g.astype(jnp.float32) + b.astype(jnp.float32)).astype(x.dtype)


def l2_normalize(x, eps=1e-6):
    return x * lax.rsqrt(jnp.sum(x * x, axis=-1, keepdims=True) + eps)


def causal_depthwise_conv(x, buf, w):
    width = w.shape[0]
    xp = jnp.concatenate([buf.astype(x.dtype), x], axis=1)
    y = lax.conv_general_dilated(xp, w[:, None, :].astype(x.dtype), window_strides=(1,), padding="VALID",
                                 dimension_numbers=("NWC", "WIO", "NWC"), feature_group_count=x.shape[-1])
    return y, xp[:, xp.shape[1] - (width - 1):]


def gated_delta_rule(q, k, v, beta, g, s0):
    bsz, t_len = q.shape[0], q.shape[1]
    c = min(CHUNK, t_len)
    n_chunks = -(-t_len // c)
    pad = n_chunks * c - t_len

    def prep(a):
        a = a.astype(jnp.float32)
        a = jnp.pad(a, [(0, 0), (0, pad)] + [(0, 0)] * (a.ndim - 2))
        a = a.reshape((bsz, n_chunks, c) + a.shape[2:])
        perm = (1, 0, 3, 2) + tuple(range(4, a.ndim))
        return jnp.transpose(a, perm)

    qc, kc, vc, bc, gc = prep(q), prep(k), prep(v), prep(beta), prep(g)
    gc = jnp.cumsum(gc, axis=-1)
    causal = jnp.tril(jnp.ones((c, c), dtype=bool))
    strict = jnp.tril(jnp.ones((c, c), dtype=bool), k=-1)
    eye = jnp.eye(c, dtype=jnp.float32)

    def step(state, inp):
        q_i, k_i, v_i, b_i, g_i = inp
        diff = g_i[..., :, None] - g_i[..., None, :]
        decay = jnp.where(causal, jnp.exp(jnp.where(causal, diff, 0.0)), 0.0)
        kb = k_i * b_i[..., None]
        kk = jnp.einsum("bhik,bhjk->bhij", kb, k_i)
        a_mat = eye + jnp.where(strict, kk * decay, 0.0)
        rhs = jnp.concatenate([v_i * b_i[..., None], kb * jnp.exp(g_i)[..., None]], axis=-1)
        sol = lax.linalg.triangular_solve(a_mat, rhs, left_side=True, lower=True, unit_diagonal=True)
        u, w = sol[..., :HEAD_V], sol[..., HEAD_V:]
        v_new = u - jnp.einsum("bhck,bhkv->bhcv", w, state)
        qk = jnp.einsum("bhik,bhjk->bhij", q_i, k_i) * decay
        o = (jnp.einsum("bhck,bhkv->bhcv", q_i * jnp.exp(g_i)[..., None], state)
             + jnp.einsum("bhij,bhjv->bhiv", qk, v_new))
        g_last = g_i[..., -1]
        k_dec = k_i * jnp.exp(g_last[..., None] - g_i)[..., None]
        state = state * jnp.exp(g_last)[..., None, None] + jnp.einsum("bhck,bhcv->bhkv", k_dec, v_new)
        return state, o

    s_final, o = lax.scan(step, s0.astype(jnp.float32), (qc, kc, vc, bc, gc))
    o = jnp.transpose(o, (1, 0, 3, 2, 4)).reshape(bsz, n_chunks * c, N_HEADS, HEAD_V)[:, :t_len]
    return o, s_final


def hybrid_layer(x, c, conf_buf, short_buf, s0, w_ada, b_ada, norm1_g, w_in, conf_dw_w, conf_dw_b,
                 conf_ln_g, conf_ln_b, w_conf_out, short_conv_w, a_log, dt_bias, delta_norm_g,
                 w_delta_out, w_merge_out, norm2_g, w_ffn_in, w_ffn_out):
    bsz, t_len, _ = x.shape
    mod = (jax.nn.silu(c) @ w_ada + b_ada)[:, None, :]
    sh1, sc1, gt1, sh2, sc2, gt2 = jnp.split(mod, 6, axis=-1)
    h = rms_norm(x, norm1_g) * (1 + sc1) + sh1
    proj = h @ w_in
    glu_a, glu_b, qkv, z, b_raw, a_raw, m_a, m_b = jnp.split(
        proj, [O_GLU_A, O_GLU_B, O_QKV, O_Z, O_BETA, O_ALPHA, O_MERGE_A], axis=-1)
    u = glu_a * jax.nn.sigmoid(glu_b)
    ca, conf_buf_new = causal_depthwise_conv(u, conf_buf, conf_dw_w)
    ca = jax.nn.silu(layer_norm(ca + conf_dw_b, conf_ln_g, conf_ln_b))
    y_a = ca @ w_conf_out
    qkv_c, short_buf_new = causal_depthwise_conv(qkv, short_buf, short_conv_w)
    qkv_c = jax.nn.silu(qkv_c).astype(jnp.float32)
    q, k, v = jnp.split(qkv_c, [QK_DIM, 2 * QK_DIM], axis=-1)
    q = l2_normalize(q.reshape(bsz, t_len, N_HEADS, HEAD_K)) * (HEAD_K ** -0.5)
    k = l2_normalize(k.reshape(bsz, t_len, N_HEADS, HEAD_K))
    v = v.reshape(bsz, t_len, N_HEADS, HEAD_V)
    beta = jax.nn.sigmoid(b_raw.astype(jnp.float32))
    g = -jnp.exp(a_log.astype(jnp.float32)) * jax.nn.softplus(a_raw.astype(jnp.float32) + dt_bias.astype(jnp.float32))
    o, s_new = gated_delta_rule(q, k, v, beta, g, s0)
    zf = z.astype(jnp.float32).reshape(bsz, t_len, N_HEADS, HEAD_V)
    o = (o * lax.rsqrt(jnp.mean(o * o, axis=-1, keepdims=True) + RMS_EPS)
         * delta_norm_g.astype(jnp.float32) * jax.nn.silu(zf))
    y_b = o.reshape(bsz, t_len, V_DIM).astype(x.dtype) @ w_delta_out
    merged = jax.nn.sigmoid(m_a) * y_a + jax.nn.sigmoid(m_b) * y_b
    x = x + gt1 * (merged @ w_merge_out)
    h2 = rms_norm(x, norm2_g) * (1 + sc2) + sh2
    gate, up = jnp.split(h2 @ w_ffn_in, 2, axis=-1)
    x = x + gt2 * ((jax.nn.silu(gate) * up) @ w_ffn_out)
    return x, conf_buf_new, short_buf_new, s_new


def setup_inputs(seed: int = 0) -> dict:
    key = jax.random.key(seed)
    ks = jax.random.split(key, 32)
    D = D_MODEL

    def nrm(k, shape, scale):
        return jax.random.normal(k, shape, jnp.float32) * scale

    dt = jnp.exp(jax.random.uniform(ks[17], (DEPTH, N_HEADS), jnp.float32, math.log(1e-3), math.log(1e-1)))
    return {
        "x_prompt": nrm(ks[0], (BATCH, SEQ, D), 1.0),
        "x_sample": nrm(ks[1], (DEC_BATCH, DEC_SEQ, D), 1.0),
        "c_prompt": nrm(ks[2], (BATCH, D), 1.0),
        "c_sample": nrm(ks[3], (DEC_BATCH, D), 1.0),
        "state_conformer_conv": nrm(ks[4], (DEPTH, DEC_BATCH, CONF_KW - 1, CONF_CH), 0.5),
        "state_short_conv": nrm(ks[5], (DEPTH, DEC_BATCH, SHORT_KW - 1, QKV_DIM), 1.0),
        "state_delta": nrm(ks[6], (DEPTH, DEC_BATCH, N_HEADS, HEAD_K, HEAD_V), 0.05),
        "w_ada": nrm(ks[7], (DEPTH, D, 6 * D), 0.5 * D ** -0.5),
        "b_ada": nrm(ks[8], (DEPTH, 6 * D), 0.01),
        "norm1_g": 1.0 + nrm(ks[9], (DEPTH, D), 0.02),
        "w_in": nrm(ks[10], (DEPTH, D, IN_DIM), D ** -0.5),
        "conf_dw_w": nrm(ks[11], (DEPTH, CONF_KW, CONF_CH), CONF_KW ** -0.5),
        "conf_dw_b": nrm(ks[12], (DEPTH, CONF_CH), 0.01),
        "conf_ln_g": 1.0 + nrm(ks[13], (DEPTH, CONF_CH), 0.02),
        "conf_ln_b": nrm(ks[14], (DEPTH, CONF_CH), 0.01),
        "w_conf_out": nrm(ks[15], (DEPTH, CONF_CH, D), CONF_CH ** -0.5),
        "short_conv_w": nrm(ks[16], (DEPTH, SHORT_KW, QKV_DIM), SHORT_KW ** -0.5),
        "a_log": jnp.log(jax.random.uniform(ks[18], (DEPTH, N_HEADS), jnp.float32, 1.0, 16.0)),
        "dt_bias": dt + jnp.log(-jnp.expm1(-dt)),
        "delta_norm_g": 1.0 + nrm(ks[19], (DEPTH, HEAD_V), 0.02),
        "w_delta_out": nrm(ks[20], (DEPTH, V_DIM, D), V_DIM ** -0.5),
        "w_merge_out": nrm(ks[21], (DEPTH, D, D), D ** -0.5),
        "norm2_g": 1.0 + nrm(ks[22], (DEPTH, D), 0.02),
        "w_ffn_in": nrm(ks[23], (DEPTH, D, 2 * D_FF), D ** -0.5),
        "w_ffn_out": nrm(ks[24], (DEPTH, D_FF, D), D_FF ** -0.5),
        "final_norm_g": 1.0 + nrm(ks[25], (D,), 0.02),
    }


def reference(x_prompt, x_sample, c_prompt, c_sample, state_conformer_conv, state_short_conv, state_delta,
              w_ada, b_ada, norm1_g, w_in, conf_dw_w, conf_dw_b, conf_ln_g, conf_ln_b, w_conf_out,
              short_conv_w, a_log, dt_bias, delta_norm_g, w_delta_out, w_merge_out, norm2_g,
              w_ffn_in, w_ffn_out, final_norm_g):
    bp = x_prompt.shape[0]
    xp, xs = x_prompt, x_sample
    conf_p, conf_s, short_p, short_s, delta_p, delta_s = [], [], [], [], [], []
    for l in range(DEPTH):
        weights = (w_ada[l], b_ada[l], norm1_g[l], w_in[l], conf_dw_w[l], conf_dw_b[l], conf_ln_g[l],
                   conf_ln_b[l], w_conf_out[l], short_conv_w[l], a_log[l], dt_bias[l], delta_norm_g[l],
                   w_delta_out[l], w_merge_out[l], norm2_g[l], w_ffn_in[l], w_ffn_out[l])
        xp, cb, sb, ds = hybrid_layer(
            xp, c_prompt,
            jnp.zeros((bp, CONF_KW - 1, CONF_CH), x_prompt.dtype),
            jnp.zeros((bp, SHORT_KW - 1, QKV_DIM), x_prompt.dtype),
            jnp.zeros((bp, N_HEADS, HEAD_K, HEAD_V), jnp.float32),
            *weights)
        conf_p.append(cb)
        short_p.append(sb)
        delta_p.append(ds)
        xs, cb, sb, ds = hybrid_layer(
            xs, c_sample, state_conformer_conv[l], state_short_conv[l], state_delta[l], *weights)
        conf_s.append(cb)
        short_s.append(sb)
        delta_s.append(ds)
    y_prompt = rms_norm(xp, final_norm_g)
    y_sample = rms_norm(xs, final_norm_g)
    return (y_prompt, y_sample, jnp.stack(conf_p), jnp.stack(conf_s), jnp.stack(short_p),
            jnp.stack(short_s), jnp.stack(delta_p), jnp.stack(delta_s))
```

```python
import functools

import jax
import jax.numpy as jnp
from jax import lax
from jax.experimental import pallas as pl
from jax.experimental.pallas import tpu as pltpu

F32 = jnp.float32
BF16 = jnp.bfloat16

CONF_KW = 31
SHORT_KW = 4
N_HEADS = 4
HEAD_DIM = 128
RMS_EPS = 1e-6
LN_EPS = 1e-5
L2_EPS = 1e-6
DELTA_CHUNK = 64
LANES = 128
CONF_TAIL = 32
SHORT_TAIL = 8
VMEM_LIMIT = 56 * 1024 * 1024


def _bdot(a, b):
    return jnp.dot(a.astype(BF16), b.astype(BF16), preferred_element_type=F32)


def _bdot_nt(a, b):
    return lax.dot_general(a.astype(BF16), b.astype(BF16), (((1,), (1,)), ((), ())),
                           preferred_element_type=F32)


def _silu(x):
    return x * jax.nn.sigmoid(x)


def _softplus(x):
    return jnp.maximum(x, 0.0) + jnp.log1p(jnp.exp(-jnp.abs(x)))


def _const_spec(shape, index_map):
    return pl.BlockSpec(shape, index_map, pipeline_mode=pl.Buffered(1))


def _params(n_axes):
    return pltpu.CompilerParams(dimension_semantics=("arbitrary",) * n_axes,
                                vmem_limit_bytes=VMEM_LIMIT)


def _ada_kernel(c_ref, w_ref, b_ref, o_ref):
    c = c_ref[...]
    o_ref[...] = _bdot(_silu(c), w_ref[...]) + b_ref[...]


def _ada_call(c_all, w_ada, b_ada):
    depth, d, six_d = w_ada.shape
    rows = c_all.shape[0]
    tn = 1536
    return pl.pallas_call(
        _ada_kernel,
        grid=(depth, six_d // tn),
        in_specs=[
            pl.BlockSpec((rows, d), lambda l, j: (0, 0)),
            pl.BlockSpec((None, d, tn), lambda l, j: (l, 0, j)),
            pl.BlockSpec((None, 1, tn), lambda l, j: (l, 0, j)),
        ],
        out_specs=pl.BlockSpec((None, rows, tn), lambda l, j: (l, 0, j)),
        out_shape=jax.ShapeDtypeStruct((depth, rows, six_d), F32),
        compiler_params=_params(2),
        name="ada_mod",
    )(c_all, w_ada, b_ada.reshape(depth, 1, six_d))


def _inproj_kernel(x_ref, sc_ref, sh_ref, g_ref, wglu_ref, wqkv_ref, wz_ref, wba_ref, wm_ref,
                   u_ref, qkv_ref, z_ref, ba_ref, gate_ref, *, conf_ch):
    x = x_ref[...]
    h = x * lax.rsqrt(jnp.mean(x * x, axis=-1, keepdims=True) + RMS_EPS) * g_ref[...]
    h = h * (1.0 + sc_ref[...]) + sh_ref[...]
    hb = h.astype(BF16)
    glu = jnp.dot(hb, wglu_ref[...], preferred_element_type=F32)
    u_ref[...] = glu[:, :conf_ch] * jax.nn.sigmoid(glu[:, conf_ch:])
    qkv_ref[...] = jnp.dot(hb, wqkv_ref[...], preferred_element_type=F32)
    z_ref[...] = jnp.dot(hb, wz_ref[...], preferred_element_type=F32)
    ba_ref[...] = jnp.dot(hb, wba_ref[...], preferred_element_type=F32)
    gate_ref[...] = jax.nn.sigmoid(jnp.dot(hb, wm_ref[...], preferred_element_type=F32))


def _mod_spec(mod, tm, col):
    d = mod.shape[-1] // 6
    if mod.shape[1] == 1:
        return pl.BlockSpec((None, 1, d), lambda g, i: (g, 0, col))
    return pl.BlockSpec((None, tm, d), lambda g, i: (g, i, col))


def _inproj_call(x, mod, norm_g, wglu, wqkv, wz, wba, wm, layer, tm):
    groups, rows, d = x.shape
    conf_ch = wglu.shape[-1] // 2
    widths = (conf_ch, wqkv.shape[-1], wz.shape[-1], wba.shape[-1], wm.shape[-1])

    def wspec(w):
        return _const_spec((None,) + w.shape[1:], lambda g, i: (layer, 0, 0))

    return pl.pallas_call(
        functools.partial(_inproj_kernel, conf_ch=conf_ch),
        grid=(groups, rows // tm),
        in_specs=[
            pl.BlockSpec((None, tm, d), lambda g, i: (g, i, 0)),
            _mod_spec(mod, tm, 1),
            _mod_spec(mod, tm, 0),
            wspec(norm_g), wspec(wglu), wspec(wqkv), wspec(wz), wspec(wba), wspec(wm),
        ],
        out_specs=[pl.BlockSpec((None, tm, w), lambda g, i: (g, i, 0)) for w in widths],
        out_shape=[jax.ShapeDtypeStruct((groups, rows, w), F32) for w in widths],
        compiler_params=_params(2),
        name="in_proj",
    )(x, mod, mod, norm_g, wglu, wqkv, wz, wba, wm)


def _layer_norm_swish(ca, g, b):
    mu = jnp.mean(ca, axis=-1, keepdims=True)
    cen = ca - mu
    var = jnp.mean(cen * cen, axis=-1, keepdims=True)
    return _silu(cen * lax.rsqrt(var + LN_EPS) * g + b)


def _conf_seq_kernel(u_ref, w_ref, b_ref, lng_ref, lnb_ref, wout_ref, ya_ref, buf_ref, ext_ref, ca_ref,
                     *, tm, rc):
    t = pl.program_id(1)
    hist = CONF_KW - 1

    @pl.when(t == 0)
    def _():
        ext_ref[0:CONF_TAIL, :] = jnp.zeros((CONF_TAIL, ext_ref.shape[1]), F32)

    @pl.when(t > 0)
    def _():
        ext_ref[0:CONF_TAIL, :] = ext_ref[tm:tm + CONF_TAIL, :]

    ext_ref[CONF_TAIL:CONF_TAIL + tm, :] = u_ref[...]

    for r0 in range(0, tm, rc):
        acc = jnp.zeros((rc, ext_ref.shape[1]), F32)
        for j in range(CONF_KW):
            start = r0 + (CONF_TAIL - hist) + j
            acc = acc + ext_ref[start:start + rc, :] * w_ref[j:j + 1, :]
        ca_ref[r0:r0 + rc, :] = acc
    act = _layer_norm_swish(ca_ref[...] + b_ref[...], lng_ref[...], lnb_ref[...])
    ya_ref[...] = _bdot(act, wout_ref[...])

    @pl.when(t == pl.num_programs(1) - 1)
    def _():
        buf_ref[...] = ext_ref[tm + CONF_TAIL - hist:tm + CONF_TAIL, :]


def _conf_seq_call(u, dw_w, dw_b, ln_g, ln_b, wout, layer, tm):
    bsz, t_len, ch = u.shape
    d = wout.shape[-1]
    hist = CONF_KW - 1
    assert t_len % tm == 0 and tm >= CONF_TAIL

    def wspec(w):
        return _const_spec((None,) + w.shape[1:], lambda b, t: (layer, 0, 0))

    return pl.pallas_call(
        functools.partial(_conf_seq_kernel, tm=tm, rc=32),
        grid=(bsz, t_len // tm),
        in_specs=[pl.BlockSpec((None, tm, ch), lambda b, t: (b, t, 0)),
                  wspec(dw_w), wspec(dw_b), wspec(ln_g), wspec(ln_b), wspec(wout)],
        out_specs=[pl.BlockSpec((None, tm, d), lambda b, t: (b, t, 0)),
                   pl.BlockSpec((None, hist, ch), lambda b, t: (b, 0, 0))],
        out_shape=[jax.ShapeDtypeStruct((bsz, t_len, d), F32),
                   jax.ShapeDtypeStruct((bsz, hist, ch), F32)],
        scratch_shapes=[pltpu.VMEM((tm + CONF_TAIL, ch), F32), pltpu.VMEM((tm, ch), F32)],
        compiler_params=_params(2),
        name="conformer_seq",
    )(u, dw_w, dw_b, ln_g, ln_b, wout)


def _conf_step_kernel(u_ref, st_ref, w_ref, b_ref, lng_ref, lnb_ref, wout_ref, ya_ref, nst_ref):
    ch = u_ref.shape[1]
    hist = CONF_KW - 1
    u = u_ref[...]
    acc = u * w_ref[hist:hist + 1, :]
    for j in range(hist):
        acc = acc + st_ref[:, j * ch:(j + 1) * ch] * w_ref[j:j + 1, :]
    act = _layer_norm_swish(acc + b_ref[...], lng_ref[...], lnb_ref[...])
    ya_ref[...] = _bdot(act, wout_ref[...])
    nst_ref[:, 0:(hist - 1) * ch] = st_ref[:, ch:hist * ch]
    nst_ref[:, (hist - 1) * ch:hist * ch] = u


def _conf_step_call(u, st, dw_w, dw_b, ln_g, ln_b, wout, layer, bb):
    bsz, ch = u.shape
    d = wout.shape[-1]
    flat = st.shape[-1]

    def wspec(w):
        return _const_spec((None,) + w.shape[1:], lambda i: (layer, 0, 0))

    return pl.pallas_call(
        _conf_step_kernel,
        grid=(bsz // bb,),
        in_specs=[pl.BlockSpec((bb, ch), lambda i: (i, 0)),
                  pl.BlockSpec((None, bb, flat), lambda i: (layer, i, 0)),
                  wspec(dw_w), wspec(dw_b), wspec(ln_g), wspec(ln_b), wspec(wout)],
        out_specs=[pl.BlockSpec((bb, d), lambda i: (i, 0)),
                   pl.BlockSpec((bb, flat), lambda i: (i, 0))],
        out_shape=[jax.ShapeDtypeStruct((bsz, d), F32),
                   jax.ShapeDtypeStruct((bsz, flat), F32)],
        compiler_params=_params(1),
        name="conformer_step",
    )(u, st, dw_w, dw_b, ln_g, ln_b, wout)


def _head_column(x, lane):
    ids = lax.broadcasted_iota(jnp.int32, x.shape, 1)
    return jnp.sum(jnp.where(ids == lane, x, 0.0), axis=-1, keepdims=True)


def _expand_heads(x, first_lane):
    rows = x.shape[0]
    return jnp.concatenate(
        [jnp.broadcast_to(_head_column(x, first_lane + h), (rows, HEAD_DIM)) for h in range(N_HEADS)], axis=1)


def _l2_normalize_heads(x):
    outs = []
    for h in range(N_HEADS):
        xh = x[:, h * HEAD_DIM:(h + 1) * HEAD_DIM]
        outs.append(xh * lax.rsqrt(jnp.sum(xh * xh, axis=-1, keepdims=True) + L2_EPS))
    return jnp.concatenate(outs, axis=1)


def _gated_out_norm(o, z, ng):
    outs = []
    for h in range(N_HEADS):
        oh = o[:, h * HEAD_DIM:(h + 1) * HEAD_DIM]
        outs.append(oh * lax.rsqrt(jnp.mean(oh * oh, axis=-1, keepdims=True) + RMS_EPS))
    return jnp.concatenate(outs, axis=1) * ng * _silu(z)


def _beta_and_log_decay(ba, alog, dtb):
    beta = jax.nn.sigmoid(ba)
    g = -jnp.exp(alog) * _softplus(ba + dtb)
    return beta, g


def _delta_seq_kernel(qkv_ref, z_ref, ba_ref, scw_ref, alog_ref, dtb_ref, ng_ref, wout_ref,
                      yb_ref, sbuf_ref, state_ref,
                      ext_ref, o_ref, q_ref, k_ref, kb_ref, rhs_ref, qg_ref, gx_ref,
                      lmat_ref, qk_ref, lt_ref, nt_ref, nmat_ref, *, tm):
    t = pl.program_id(1)
    hist = SHORT_KW - 1
    qk_dim = N_HEADS * HEAD_DIM
    cl = DELTA_CHUNK
    n_chunks = tm // cl
    n_sys = n_chunks * N_HEADS

    @pl.when(t == 0)
    def _():
        ext_ref[0:SHORT_TAIL, :] = jnp.zeros((SHORT_TAIL, ext_ref.shape[1]), F32)
        state_ref[...] = jnp.zeros(state_ref.shape, F32)

    @pl.when(t > 0)
    def _():
        ext_ref[0:SHORT_TAIL, :] = ext_ref[tm:tm + SHORT_TAIL, :]

    ext_ref[SHORT_TAIL:SHORT_TAIL + tm, :] = qkv_ref[...]
    conv = ext_ref[SHORT_TAIL - hist:SHORT_TAIL - hist + tm, :] * scw_ref[0:1, :]
    for j in range(1, SHORT_KW):
        conv = conv + ext_ref[SHORT_TAIL - hist + j:SHORT_TAIL - hist + j + tm, :] * scw_ref[j:j + 1, :]
    act = _silu(conv)
    q = _l2_normalize_heads(act[:, 0:qk_dim]) * (HEAD_DIM ** -0.5)
    k = _l2_normalize_heads(act[:, qk_dim:2 * qk_dim])
    v = act[:, 2 * qk_dim:]

    beta, g = _beta_and_log_decay(ba_ref[...], alog_ref[...], dtb_ref[...])
    pos = lax.broadcasted_iota(jnp.int32, g.shape, 0) % cl
    shift = 1
    while shift < cl:
        g = g + jnp.where(pos >= shift, pltpu.roll(g, shift, 0), 0.0)
        shift *= 2
    beta_x = _expand_heads(beta, 0)
    g_x = _expand_heads(g, N_HEADS)
    eg_x = jnp.exp(g_x)
    kb = k * beta_x
    q_ref[...] = q
    k_ref[...] = k
    kb_ref[...] = kb
    qg_ref[...] = q * eg_x
    gx_ref[...] = g_x
    rhs_u = v * beta_x
    rhs_w = kb * eg_x
    for h in range(N_HEADS):
        hs = slice(h * HEAD_DIM, (h + 1) * HEAD_DIM)
        rhs_ref[:, 2 * h * HEAD_DIM:(2 * h + 1) * HEAD_DIM] = rhs_u[:, hs]
        rhs_ref[:, (2 * h + 1) * HEAD_DIM:(2 * h + 2) * HEAD_DIM] = rhs_w[:, hs]

    row = lax.broadcasted_iota(jnp.int32, (cl, cl), 0)
    col = lax.broadcasted_iota(jnp.int32, (cl, cl), 1)
    causal = row >= col
    strict = row > col

    def build(c, carry):
        r0 = pl.multiple_of(c * cl, cl)
        for h in range(N_HEADS):
            hs = slice(h * HEAD_DIM, (h + 1) * HEAD_DIM)
            g_i = gx_ref[pl.ds(r0, cl), hs]
            g_j = g_i.T[0:cl, :]
            decay = jnp.where(causal, jnp.exp(jnp.where(causal, g_i[:, 0:cl] - g_j, 0.0)), 0.0)
            k_c = k_ref[pl.ds(r0, cl), hs]
            kk = _bdot_nt(kb_ref[pl.ds(r0, cl), hs], k_c)
            s0 = pl.multiple_of((c * N_HEADS + h) * cl, cl)
            lmat_ref[pl.ds(s0, cl), :] = jnp.where(strict, kk * decay, 0.0)
            qk_ref[pl.ds(s0, cl), :] = _bdot_nt(q_ref[pl.ds(r0, cl), hs], k_c) * decay
        return carry

    lax.fori_loop(0, n_chunks, build, 0)

    for i in range(1, cl):
        lt_ref[i] = lmat_ref[pl.ds(i, n_sys, stride=cl), :].T
    nt_ref[...] = jnp.zeros(nt_ref.shape, F32)
    sub = 8
    for i in range(1, cl):
        acc = [-lt_ref[i, sub * r:sub * (r + 1), :] for r in range((i - 1) // sub + 1)]
        for j in range(1, i):
            l_ij = lt_ref[i, j:j + 1, :]
            for r in range((j - 1) // sub + 1):
                acc[r] = acc[r] - l_ij * nt_ref[j, sub * r:sub * (r + 1), :]
        for r, a in enumerate(acc):
            nt_ref[i, sub * r:sub * (r + 1), :] = a
    for i in range(cl):
        nmat_ref[pl.ds(i, n_sys, stride=cl), :] = nt_ref[i].T

    def step(c, carry):
        r0 = pl.multiple_of(c * cl, cl)
        for h in range(N_HEADS):
            hs = slice(h * HEAD_DIM, (h + 1) * HEAD_DIM)
            s0 = pl.multiple_of((c * N_HEADS + h) * cl, cl)
            rhs = rhs_ref[pl.ds(r0, cl), 2 * h * HEAD_DIM:(2 * h + 2) * HEAD_DIM]
            sol = rhs + jnp.dot(nmat_ref[pl.ds(s0, cl), :], rhs, precision=lax.Precision.HIGHEST,
                                preferred_element_type=F32)
            s_old = state_ref[h]
            v_new = sol[:, 0:HEAD_DIM] - _bdot(sol[:, HEAD_DIM:], s_old)
            o_ref[pl.ds(r0, cl), hs] = (_bdot(qg_ref[pl.ds(r0, cl), hs], s_old)
                                        + _bdot(qk_ref[pl.ds(s0, cl), :], v_new))
            g_i = gx_ref[pl.ds(r0, cl), hs]
            g_last = g_i[cl - 1:cl, :]
            k_dec = k_ref[pl.ds(r0, cl), hs] * jnp.exp(g_last - g_i)
            state_ref[h] = s_old * jnp.exp(g_last) + _bdot(k_dec.T, v_new)
        return carry

    lax.fori_loop(0, n_chunks, step, 0)

    yb_ref[...] = _bdot(_gated_out_norm(o_ref[...], z_ref[...], ng_ref[...]), wout_ref[...])

    @pl.when(t == pl.num_programs(1) - 1)
    def _():
        sbuf_ref[...] = ext_ref[tm + SHORT_TAIL - hist:tm + SHORT_TAIL, :]


def _delta_seq_call(qkv, z, ba, scw, alog, dtb, ng, wout, layer, tm):
    bsz, t_len, qkv_dim = qkv.shape
    v_dim = z.shape[-1]
    d = wout.shape[-1]
    hist = SHORT_KW - 1
    assert t_len % tm == 0 and tm % DELTA_CHUNK == 0
    cl = DELTA_CHUNK
    n_sys = (tm // cl) * N_HEADS

    def wspec(w):
        return _const_spec((None,) + w.shape[1:], lambda b, t: (layer, 0, 0))

    return pl.pallas_call(
        functools.partial(_delta_seq_kernel, tm=tm),
        grid=(bsz, t_len // tm),
        in_specs=[pl.BlockSpec((None, tm, qkv_dim), lambda b, t: (b, t, 0)),
                  pl.BlockSpec((None, tm, v_dim), lambda b, t: (b, t, 0)),
                  pl.BlockSpec((None, tm, LANES), lambda b, t: (b, t, 0)),
                  wspec(scw), wspec(alog), wspec(dtb), wspec(ng), wspec(wout)],
        out_specs=[pl.BlockSpec((None, tm, d), lambda b, t: (b, t, 0)),
                   pl.BlockSpec((None, hist, qkv_dim), lambda b, t: (b, 0, 0)),
                   pl.BlockSpec((None, N_HEADS, HEAD_DIM, HEAD_DIM), lambda b, t: (b, 0, 0, 0))],
        out_shape=[jax.ShapeDtypeStruct((bsz, t_len, d), F32),
                   jax.ShapeDtypeStruct((bsz, hist, qkv_dim), F32),
                   jax.ShapeDtypeStruct((bsz, N_HEADS, HEAD_DIM, HEAD_DIM), F32)],
        scratch_shapes=[pltpu.VMEM((tm + SHORT_TAIL, qkv_dim), F32),
                        pltpu.VMEM((tm, v_dim), F32),
                        pltpu.VMEM((tm, v_dim), F32),
                        pltpu.VMEM((tm, v_dim), F32),
                        pltpu.VMEM((tm, v_dim), F32),
                        pltpu.VMEM((tm, 2 * v_dim), F32),
                        pltpu.VMEM((tm, v_dim), F32),
                        pltpu.VMEM((tm, v_dim), F32),
                        pltpu.VMEM((n_sys * cl, cl), F32),
                        pltpu.VMEM((n_sys * cl, cl), F32),
                        pltpu.VMEM((cl, cl, n_sys), F32),
                        pltpu.VMEM((cl, cl, n_sys), F32),
                        pltpu.VMEM((n_sys * cl, cl), F32)],
        compiler_params=_params(2),
        name="delta_seq",
    )(qkv, z, ba, scw, alog, dtb, ng, wout)


def _delta_step_kernel(qkv_ref, z_ref, ba_ref, sst_ref, ds_ref, scw_ref, alog_ref, dtb_ref, ng_ref, wout_ref,
                       yb_ref, nsst_ref, nds_ref, o_ref):
    bb, qkv_dim = qkv_ref.shape
    hist = SHORT_KW - 1
    qk_dim = N_HEADS * HEAD_DIM
    x = qkv_ref[...]
    conv = x * scw_ref[hist:hist + 1, :]
    for j in range(hist):
        conv = conv + sst_ref[:, j * qkv_dim:(j + 1) * qkv_dim] * scw_ref[j:j + 1, :]
    nsst_ref[:, 0:(hist - 1) * qkv_dim] = sst_ref[:, qkv_dim:hist * qkv_dim]
    nsst_ref[:, (hist - 1) * qkv_dim:hist * qkv_dim] = x
    act = _silu(conv)
    q = _l2_normalize_heads(act[:, 0:qk_dim]) * (HEAD_DIM ** -0.5)
    k = _l2_normalize_heads(act[:, qk_dim:2 * qk_dim])
    v = act[:, 2 * qk_dim:]
    beta, g = _beta_and_log_decay(ba_ref[...], alog_ref[...], dtb_ref[...])
    beta_x = _expand_heads(beta, 0)
    eg_x = jnp.exp(_expand_heads(g, N_HEADS))
    kb = k * beta_x
    u_all = v * beta_x
    w_all = kb * eg_x
    qg = q * eg_x
    sub = lax.broadcasted_iota(jnp.int32, (8, HEAD_DIM), 0)
    for b in range(bb):
        for h in range(N_HEADS):
            hs = slice(h * HEAD_DIM, (h + 1) * HEAD_DIM)
            s_old = ds_ref[b, h]
            lhs = jnp.where(sub == 0, w_all[b:b + 1, hs], jnp.where(sub == 1, qg[b:b + 1, hs], 0.0))
            prod = _bdot(lhs, s_old)
            v_new = u_all[b:b + 1, hs] - prod[0:1, :]
            k_row = k[b:b + 1, hs]
            qk = jnp.sum(q[b:b + 1, hs] * k_row, axis=-1, keepdims=True)
            o_ref[b:b + 1, hs] = prod[1:2, :] + qk * v_new
            k_col = jnp.broadcast_to(k_row, (HEAD_DIM, HEAD_DIM)).T
            nds_ref[b, h] = s_old * eg_x[b:b + 1, hs] + k_col * v_new
    yb_ref[...] = _bdot(_gated_out_norm(o_ref[...], z_ref[...], ng_ref[...]), wout_ref[...])


def _delta_step_call(qkv, z, ba, sst, ds, scw, alog, dtb, ng, wout, layer, bb):
    bsz, qkv_dim = qkv.shape
    v_dim = z.shape[-1]
    d = wout.shape[-1]
    flat = sst.shape[-1]

    def wspec(w):
        return _const_spec((None,) + w.shape[1:], lambda i: (layer, 0, 0))

    state_block = (bb, N_HEADS, HEAD_DIM, HEAD_DIM)
    return pl.pallas_call(
        _delta_step_kernel,
        grid=(bsz // bb,),
        in_specs=[pl.BlockSpec((bb, qkv_dim), lambda i: (i, 0)),
                  pl.BlockSpec((bb, v_dim), lambda i: (i, 0)),
                  pl.BlockSpec((bb, LANES), lambda i: (i, 0)),
                  pl.BlockSpec((None, bb, flat), lambda i: (layer, i, 0)),
                  pl.BlockSpec((None,) + state_block, lambda i: (layer, i, 0, 0, 0)),
                  wspec(scw), wspec(alog), wspec(dtb), wspec(ng), wspec(wout)],
        out_specs=[pl.BlockSpec((bb, d), lambda i: (i, 0)),
                   pl.BlockSpec((bb, flat), lambda i: (i, 0)),
                   pl.BlockSpec(state_block, lambda i: (i, 0, 0, 0))],
        out_shape=[jax.ShapeDtypeStruct((bsz, d), F32),
                   jax.ShapeDtypeStruct((bsz, flat), F32),
                   jax.ShapeDtypeStruct((bsz, N_HEADS, HEAD_DIM, HEAD_DIM), F32)],
        scratch_shapes=[pltpu.VMEM((bb, v_dim), F32)],
        compiler_params=_params(1),
        name="delta_step",
    )(qkv, z, ba, sst, ds, scw, alog, dtb, ng, wout)


def _merge_ffn_kernel(x_ref, ya_ref, yb_ref, gate_ref, gt1_ref, sc2_ref, sh2_ref, gt2_ref, n2g_ref,
                      wm_ref, wfi_ref, wfo_ref, fin_ref, out_ref, *, ff_chunk, final_norm):
    d = x_ref.shape[-1]
    d_ff = wfo_ref.shape[0]
    merged = gate_ref[:, 0:d] * ya_ref[...] + gate_ref[:, d:2 * d] * yb_ref[...]
    x1 = x_ref[...] + gt1_ref[...] * _bdot(merged, wm_ref[...])
    h2 = x1 * lax.rsqrt(jnp.mean(x1 * x1, axis=-1, keepdims=True) + RMS_EPS) * n2g_ref[...]
    hb = (h2 * (1.0 + sc2_ref[...]) + sh2_ref[...]).astype(BF16)
    acc = jnp.zeros(x1.shape, F32)
    for c0 in range(0, d_ff, ff_chunk):
        gate = jnp.dot(hb, wfi_ref[:, c0:c0 + ff_chunk], preferred_element_type=F32)
        up = jnp.dot(hb, wfi_ref[:, d_ff + c0:d_ff + c0 + ff_chunk], preferred_element_type=F32)
        acc = acc + _bdot(_silu(gate) * up, wfo_ref[c0:c0 + ff_chunk, :])
    x2 = x1 + gt2_ref[...] * acc
    if final_norm:
        x2 = x2 * lax.rsqrt(jnp.mean(x2 * x2, axis=-1, keepdims=True) + RMS_EPS) * fin_ref[...]
    out_ref[...] = x2


def _merge_ffn_call(x, ya, yb, gates, mod, n2g, wm, wfi, wfo, fin_g, layer, tm, final_norm):
    groups, rows, d = x.shape
    d_ff = wfo.shape[1]
    ff_chunk = 256
    assert d_ff % ff_chunk == 0

    def wspec(w):
        return _const_spec((None,) + w.shape[1:], lambda g, i: (layer, 0, 0))

    def rowspec(w):
        return pl.BlockSpec((None, tm, w), lambda g, i: (g, i, 0))

    return pl.pallas_call(
        functools.partial(_merge_ffn_kernel, ff_chunk=ff_chunk, final_norm=final_norm),
        grid=(groups, rows // tm),
        in_specs=[rowspec(d), rowspec(d), rowspec(d), rowspec(2 * d),
                  _mod_spec(mod, tm, 2), _mod_spec(mod, tm, 4), _mod_spec(mod, tm, 3), _mod_spec(mod, tm, 5),
                  wspec(n2g), wspec(wm), wspec(wfi), wspec(wfo),
                  _const_spec((1, d), lambda g, i: (0, 0))],
        out_specs=rowspec(d),
        out_shape=jax.ShapeDtypeStruct((groups, rows, d), F32),
        compiler_params=_params(2),
        name="merge_ffn",
    )(x, ya, yb, gates, mod, mod, mod, mod, n2g, wm, wfi, wfo, fin_g)


def kernel(x_prompt, x_sample, c_prompt, c_sample, state_conformer_conv, state_short_conv, state_delta, w_ada, b_ada, norm1_g, w_in, conf_dw_w, conf_dw_b, conf_ln_g, conf_ln_b, w_conf_out, short_conv_w, a_log, dt_bias, delta_norm_g, w_delta_out, w_merge_out, norm2_g, w_ffn_in, w_ffn_out, final_norm_g):
    depth, d, in_dim = w_in.shape
    bp, t_len, _ = x_prompt.shape
    bs = x_sample.shape[0]
    assert x_sample.shape[1] == 1
    conf_ch = conf_dw_w.shape[-1]
    qkv_dim = short_conv_w.shape[-1]
    v_dim = w_delta_out.shape[1]
    assert v_dim == N_HEADS * HEAD_DIM and qkv_dim == 3 * v_dim
    assert conf_dw_w.shape[1] == CONF_KW and short_conv_w.shape[1] == SHORT_KW

    o_glu = 2 * conf_ch
    o_qkv = o_glu + qkv_dim
    o_z = o_qkv + v_dim
    o_ba = o_z + 2 * N_HEADS
    assert in_dim == o_ba + 2 * d
    w_in_b = w_in.astype(BF16)
    wglu = w_in_b[:, :, :o_glu]
    wqkv = w_in_b[:, :, o_glu:o_qkv]
    wz = w_in_b[:, :, o_qkv:o_z]
    wba = jnp.pad(w_in_b[:, :, o_z:o_ba], ((0, 0), (0, 0), (0, LANES - 2 * N_HEADS)))
    wmg = w_in_b[:, :, o_ba:]
    w_conf_out_b = w_conf_out.astype(BF16)
    w_delta_out_b = w_delta_out.astype(BF16)
    w_merge_b = w_merge_out.astype(BF16)
    w_ffn_in_b = w_ffn_in.astype(BF16)
    w_ffn_out_b = w_ffn_out.astype(BF16)

    def row3(a):
        return a.reshape(depth, 1, a.shape[-1])

    norm1 = row3(norm1_g)
    norm2 = row3(norm2_g)
    dw_b = row3(conf_dw_b)
    ln_g = row3(conf_ln_g)
    ln_b = row3(conf_ln_b)
    alog = jnp.pad(a_log, ((0, 0), (N_HEADS, LANES - 2 * N_HEADS))).reshape(depth, 1, LANES)
    dtb = jnp.pad(dt_bias, ((0, 0), (N_HEADS, LANES - 2 * N_HEADS))).reshape(depth, 1, LANES)
    ng = row3(jnp.tile(delta_norm_g, (1, N_HEADS)))
    fin_g = final_norm_g.reshape(1, d)

    mod = _ada_call(jnp.concatenate([c_prompt, c_sample], axis=0), w_ada, b_ada)
    mod_p = mod[:, :bp].reshape(depth, bp, 1, 6 * d)
    mod_s = mod[:, bp:].reshape(depth, 1, bs, 6 * d)
    conf_state = state_conformer_conv.reshape(depth, bs, (CONF_KW - 1) * conf_ch)
    short_state = state_short_conv.reshape(depth, bs, (SHORT_KW - 1) * qkv_dim)

    xp = x_prompt
    xs = x_sample.reshape(1, bs, d)
    conf_p, conf_s, short_p, short_s, delta_p, delta_s = [], [], [], [], [], []
    for l in range(depth):
        last = l == depth - 1
        u, qkv, z, ba, gates = _inproj_call(xp, mod_p[l], norm1, wglu, wqkv, wz, wba, wmg, l, tm=512)
        ya, cb = _conf_seq_call(u, conf_dw_w, dw_b, ln_g, ln_b, w_conf_out_b, l, tm=256)
        yb, sb, ds = _delta_seq_call(qkv, z, ba, short_conv_w, alog, dtb, ng, w_delta_out_b, l, tm=512)
        xp = _merge_ffn_call(xp, ya, yb, gates, mod_p[l], norm2, w_merge_b, w_ffn_in_b, w_ffn_out_b, fin_g,
                             l, tm=256, final_norm=last)
        conf_p.append(cb)
        short_p.append(sb)
        delta_p.append(ds)
        u, qkv, z, ba, gates = _inproj_call(xs, mod_s[l], norm1, wglu, wqkv, wz, wba, wmg, l, tm=bs)
        ya, cb = _conf_step_call(u[0], conf_state, conf_dw_w, dw_b, ln_g, ln_b, w_conf_out_b, l, bb=32)
        yb, sb, ds = _delta_step_call(qkv[0], z[0], ba[0], short_state, state_delta, short_conv_w, alog, dtb, ng,
                                      w_delta_out_b, l, bb=8)
        xs = _merge_ffn_call(xs, ya[None], yb[None], gates, mod_s[l], norm2, w_merge_b, w_ffn_in_b, w_ffn_out_b,
                             fin_g, l, tm=bs, final_norm=last)
        conf_s.append(cb.reshape(bs, CONF_KW - 1, conf_ch))
        short_s.append(sb.reshape(bs, SHORT_KW - 1, qkv_dim))
        delta_s.append(ds)
    return (xp, xs.reshape(bs, 1, d), jnp.stack(conf_p), jnp.stack(conf_s), jnp.stack(short_p),
            jnp.stack(short_s), jnp.stack(delta_p), jnp.stack(delta_s))
```

```python
import functools

import jax
import jax.numpy as jnp
from jax import lax
from jax.experimental import pallas as pl
from jax.experimental.pallas import tpu as pltpu

F32 = jnp.float32
BF16 = jnp.bfloat16

CONF_KW = 31
SHORT_KW = 4
N_HEADS = 4
HEAD_DIM = 128
RMS_EPS = 1e-6
LN_EPS = 1e-5
L2_EPS = 1e-6
DELTA_CHUNK = 64
LANES = 128
CONF_TAIL = 32
SHORT_TAIL = 8
VMEM_LIMIT = 56 * 1024 * 1024


def _bdot(a, b):
    return jnp.dot(a.astype(BF16), b.astype(BF16), preferred_element_type=F32)


def _bdot_nt(a, b):
    return lax.dot_general(a.astype(BF16), b.astype(BF16), (((1,), (1,)), ((), ())),
                           preferred_element_type=F32)


def _silu(x):
    return x * jax.nn.sigmoid(x)


def _softplus(x):
    return jnp.maximum(x, 0.0) + jnp.log1p(jnp.exp(-jnp.abs(x)))


def _const_spec(shape, index_map):
    return pl.BlockSpec(shape, index_map, pipeline_mode=pl.Buffered(1))


def _params(n_axes):
    return pltpu.CompilerParams(dimension_semantics=("arbitrary",) * n_axes,
                                vmem_limit_bytes=VMEM_LIMIT)


def _ada_kernel(c_ref, w_ref, b_ref, o_ref):
    c = c_ref[...]
    o_ref[...] = _bdot(_silu(c), w_ref[...]) + b_ref[...]


def _ada_call(c_all, w_ada, b_ada):
    depth, d, six_d = w_ada.shape
    rows = c_all.shape[0]
    tn = 1536
    return pl.pallas_call(
        _ada_kernel,
        grid=(depth, six_d // tn),
        in_specs=[
            pl.BlockSpec((rows, d), lambda l, j: (0, 0)),
            pl.BlockSpec((None, d, tn), lambda l, j: (l, 0, j)),
            pl.BlockSpec((None, 1, tn), lambda l, j: (l, 0, j)),
        ],
        out_specs=pl.BlockSpec((None, rows, tn), lambda l, j: (l, 0, j)),
        out_shape=jax.ShapeDtypeStruct((depth, rows, six_d), F32),
        compiler_params=_params(2),
        name="ada_mod",
    )(c_all, w_ada, b_ada.reshape(depth, 1, six_d))


def _inproj_kernel(x_ref, sc_ref, sh_ref, g_ref, wglu_ref, wqkv_ref, wz_ref, wba_ref, wm_ref,
                   u_ref, qkv_ref, z_ref, ba_ref, gate_ref, *, conf_ch):
    x = x_ref[...]
    h = x * lax.rsqrt(jnp.mean(x * x, axis=-1, keepdims=True) + RMS_EPS) * g_ref[...]
    h = h * (1.0 + sc_ref[...]) + sh_ref[...]
    hb = h.astype(BF16)
    glu = jnp.dot(hb, wglu_ref[...], preferred_element_type=F32)
    u_ref[...] = glu[:, :conf_ch] * jax.nn.sigmoid(glu[:, conf_ch:])
    qkv_ref[...] = jnp.dot(hb, wqkv_ref[...], preferred_element_type=F32)
    z_ref[...] = jnp.dot(hb, wz_ref[...], preferred_element_type=F32)
    ba_ref[...] = jnp.dot(hb, wba_ref[...], preferred_element_type=F32)
    gate_ref[...] = jax.nn.sigmoid(jnp.dot(hb, wm_ref[...], preferred_element_type=F32))


def _mod_spec(mod, tm, col):
    d = mod.shape[-1] // 6
    if mod.shape[1] == 1:
        return pl.BlockSpec((None, 1, d), lambda g, i: (g, 0, col))
    return pl.BlockSpec((None, tm, d), lambda g, i: (g, i, col))


def _inproj_call(x, mod, norm_g, wglu, wqkv, wz, wba, wm, layer, tm):
    groups, rows, d = x.shape
    conf_ch = wglu.shape[-1] // 2
    widths = (conf_ch, wqkv.shape[-1], wz.shape[-1], wba.shape[-1], wm.shape[-1])

    def wspec(w):
        return _const_spec((None,) + w.shape[1:], lambda g, i: (layer, 0, 0))

    return pl.pallas_call(
        functools.partial(_inproj_kernel, conf_ch=conf_ch),
        grid=(groups, rows // tm),
        in_specs=[
            pl.BlockSpec((None, tm, d), lambda g, i: (g, i, 0)),
            _mod_spec(mod, tm, 1),
            _mod_spec(mod, tm, 0),
            wspec(norm_g), wspec(wglu), wspec(wqkv), wspec(wz), wspec(wba), wspec(wm),
        ],
        out_specs=[pl.BlockSpec((None, tm, w), lambda g, i: (g, i, 0)) for w in widths],
        out_shape=[jax.ShapeDtypeStruct((groups, rows, w), F32) for w in widths],
        compiler_params=_params(2),
        name="in_proj",
    )(x, mod, mod, norm_g, wglu, wqkv, wz, wba, wm)


def _layer_norm_swish(ca, g, b):
    mu = jnp.mean(ca, axis=-1, keepdims=True)
    cen = ca - mu
    var = jnp.mean(cen * cen, axis=-1, keepdims=True)
    return _silu(cen * lax.rsqrt(var + LN_EPS) * g + b)


def _conf_seq_kernel(u_ref, w_ref, b_ref, lng_ref, lnb_ref, wout_ref, ya_ref, buf_ref, ext_ref, ca_ref, pr_ref,
                     *, tm, rc):
    t = pl.program_id(1)
    hist = CONF_KW - 1

    @pl.when(t == 0)
    def _():
        ext_ref[0:CONF_TAIL, :] = jnp.zeros((CONF_TAIL, ext_ref.shape[1]), F32)

    @pl.when(t > 0)
    def _():
        ext_ref[0:CONF_TAIL, :] = ext_ref[tm:tm + CONF_TAIL, :]

    ext_ref[CONF_TAIL:CONF_TAIL + tm, :] = u_ref[...]

    sub = 8
    base = CONF_TAIL - hist
    for r0 in range(0, tm, rc):
        for r in range(sub):
            rows = rc + (sub if r else 0)
            acc = None
            for o in range(r if r >= base else r + sub, base + CONF_KW, sub):
                start = r0 + o - r
                term = ext_ref[start:start + rows, :] * w_ref[o - base:o - base + 1, :]
                acc = term if acc is None else acc + term
            pr_ref[r, 0:rows, :] = acc
        y = pr_ref[0, 0:rc, :]
        for r in range(1, sub):
            y = y + pr_ref[r, r:r + rc, :]
        ca_ref[r0:r0 + rc, :] = y
    act = _layer_norm_swish(ca_ref[...] + b_ref[...], lng_ref[...], lnb_ref[...])
    ya_ref[...] = _bdot(act, wout_ref[...])

    @pl.when(t == pl.num_programs(1) - 1)
    def _():
        buf_ref[...] = ext_ref[tm + CONF_TAIL - hist:tm + CONF_TAIL, :]


def _conf_seq_call(u, dw_w, dw_b, ln_g, ln_b, wout, layer, tm):
    bsz, t_len, ch = u.shape
    d = wout.shape[-1]
    hist = CONF_KW - 1
    assert t_len % tm == 0 and tm >= CONF_TAIL

    def wspec(w):
        return _const_spec((None,) + w.shape[1:], lambda b, t: (layer, 0, 0))

    rc = 64
    return pl.pallas_call(
        functools.partial(_conf_seq_kernel, tm=tm, rc=rc),
        grid=(bsz, t_len // tm),
        in_specs=[pl.BlockSpec((None, tm, ch), lambda b, t: (b, t, 0)),
                  wspec(dw_w), wspec(dw_b), wspec(ln_g), wspec(ln_b), wspec(wout)],
        out_specs=[pl.BlockSpec((None, tm, d), lambda b, t: (b, t, 0)),
                   pl.BlockSpec((None, hist, ch), lambda b, t: (b, 0, 0))],
        out_shape=[jax.ShapeDtypeStruct((bsz, t_len, d), F32),
                   jax.ShapeDtypeStruct((bsz, hist, ch), F32)],
        scratch_shapes=[pltpu.VMEM((tm + CONF_TAIL, ch), F32), pltpu.VMEM((tm, ch), F32),
                        pltpu.VMEM((8, rc + 8, ch), F32)],
        compiler_params=_params(2),
        name="conformer_seq",
    )(u, dw_w, dw_b, ln_g, ln_b, wout)


def _conf_step_kernel(u_ref, st_ref, w_ref, b_ref, lng_ref, lnb_ref, wout_ref, ya_ref, nst_ref):
    ch = u_ref.shape[1]
    hist = CONF_KW - 1
    u = u_ref[...]
    acc = u * w_ref[hist:hist + 1, :]
    for j in range(hist):
        acc = acc + st_ref[:, j * ch:(j + 1) * ch] * w_ref[j:j + 1, :]
    act = _layer_norm_swish(acc + b_ref[...], lng_ref[...], lnb_ref[...])
    ya_ref[...] = _bdot(act, wout_ref[...])
    nst_ref[:, 0:(hist - 1) * ch] = st_ref[:, ch:hist * ch]
    nst_ref[:, (hist - 1) * ch:hist * ch] = u


def _conf_step_call(u, st, dw_w, dw_b, ln_g, ln_b, wout, layer, bb):
    bsz, ch = u.shape
    d = wout.shape[-1]
    flat = st.shape[-1]

    def wspec(w):
        return _const_spec((None,) + w.shape[1:], lambda i: (layer, 0, 0))

    return pl.pallas_call(
        _conf_step_kernel,
        grid=(bsz // bb,),
        in_specs=[pl.BlockSpec((bb, ch), lambda i: (i, 0)),
                  pl.BlockSpec((None, bb, flat), lambda i: (layer, i, 0)),
                  wspec(dw_w), wspec(dw_b), wspec(ln_g), wspec(ln_b), wspec(wout)],
        out_specs=[pl.BlockSpec((bb, d), lambda i: (i, 0)),
                   pl.BlockSpec((bb, flat), lambda i: (i, 0))],
        out_shape=[jax.ShapeDtypeStruct((bsz, d), F32),
                   jax.ShapeDtypeStruct((bsz, flat), F32)],
        compiler_params=_params(1),
        name="conformer_step",
    )(u, st, dw_w, dw_b, ln_g, ln_b, wout)


def _head_column(x, lane):
    ids = lax.broadcasted_iota(jnp.int32, x.shape, 1)
    return jnp.sum(jnp.where(ids == lane, x, 0.0), axis=-1, keepdims=True)


def _expand_heads(x, first_lane):
    rows = x.shape[0]
    return jnp.concatenate(
        [jnp.broadcast_to(_head_column(x, first_lane + h), (rows, HEAD_DIM)) for h in range(N_HEADS)], axis=1)


def _l2_normalize_heads(x):
    outs = []
    for h in range(N_HEADS):
        xh = x[:, h * HEAD_DIM:(h + 1) * HEAD_DIM]
        outs.append(xh * lax.rsqrt(jnp.sum(xh * xh, axis=-1, keepdims=True) + L2_EPS))
    return jnp.concatenate(outs, axis=1)


def _gated_out_norm(o, z, ng):
    outs = []
    for h in range(N_HEADS):
        oh = o[:, h * HEAD_DIM:(h + 1) * HEAD_DIM]
        outs.append(oh * lax.rsqrt(jnp.mean(oh * oh, axis=-1, keepdims=True) + RMS_EPS))
    return jnp.concatenate(outs, axis=1) * ng * _silu(z)


def _beta_and_log_decay(ba, alog, dtb):
    beta = jax.nn.sigmoid(ba)
    g = -jnp.exp(alog) * _softplus(ba + dtb)
    return beta, g


def _delta_seq_kernel(qkv_ref, z_ref, ba_ref, scw_ref, alog_ref, dtb_ref, ng_ref, wout_ref,
                      yb_ref, sbuf_ref, state_ref,
                      ext_ref, o_ref, q_ref, k_ref, kb_ref, rhs_ref, qg_ref, gx_ref,
                      lmat_ref, qk_ref, lt_ref, nt_ref, nmat_ref, *, tm):
    t = pl.program_id(1)
    hist = SHORT_KW - 1
    qk_dim = N_HEADS * HEAD_DIM
    cl = DELTA_CHUNK
    n_chunks = tm // cl
    n_sys = n_chunks * N_HEADS

    @pl.when(t == 0)
    def _():
        ext_ref[0:SHORT_TAIL, :] = jnp.zeros((SHORT_TAIL, ext_ref.shape[1]), F32)
        state_ref[...] = jnp.zeros(state_ref.shape, F32)

    @pl.when(t > 0)
    def _():
        ext_ref[0:SHORT_TAIL, :] = ext_ref[tm:tm + SHORT_TAIL, :]

    ext_ref[SHORT_TAIL:SHORT_TAIL + tm, :] = qkv_ref[...]
    conv = ext_ref[SHORT_TAIL - hist:SHORT_TAIL - hist + tm, :] * scw_ref[0:1, :]
    for j in range(1, SHORT_KW):
        conv = conv + ext_ref[SHORT_TAIL - hist + j:SHORT_TAIL - hist + j + tm, :] * scw_ref[j:j + 1, :]
    act = _silu(conv)
    q = _l2_normalize_heads(act[:, 0:qk_dim]) * (HEAD_DIM ** -0.5)
    k = _l2_normalize_heads(act[:, qk_dim:2 * qk_dim])
    v = act[:, 2 * qk_dim:]

    beta, g = _beta_and_log_decay(ba_ref[...], alog_ref[...], dtb_ref[...])
    pos = lax.broadcasted_iota(jnp.int32, g.shape, 0) % cl
    shift = 1
    while shift < cl:
        g = g + jnp.where(pos >= shift, pltpu.roll(g, shift, 0), 0.0)
        shift *= 2
    beta_x = _expand_heads(beta, 0)
    g_x = _expand_heads(g, N_HEADS)
    eg_x = jnp.exp(g_x)
    kb = k * beta_x
    q_ref[...] = q
    k_ref[...] = k
    kb_ref[...] = kb
    qg_ref[...] = q * eg_x
    gx_ref[...] = g_x
    rhs_u = v * beta_x
    rhs_w = kb * eg_x
    for h in range(N_HEADS):
        hs = slice(h * HEAD_DIM, (h + 1) * HEAD_DIM)
        rhs_ref[:, 2 * h * HEAD_DIM:(2 * h + 1) * HEAD_DIM] = rhs_u[:, hs]
        rhs_ref[:, (2 * h + 1) * HEAD_DIM:(2 * h + 2) * HEAD_DIM] = rhs_w[:, hs]

    row = lax.broadcasted_iota(jnp.int32, (cl, cl), 0)
    col = lax.broadcasted_iota(jnp.int32, (cl, cl), 1)
    causal = row >= col
    strict = row > col

    def build(c, carry):
        r0 = pl.multiple_of(c * cl, cl)
        for h in range(N_HEADS):
            hs = slice(h * HEAD_DIM, (h + 1) * HEAD_DIM)
            g_i = gx_ref[pl.ds(r0, cl), hs]
            g_j = g_i.T[0:cl, :]
            decay = jnp.where(causal, jnp.exp(jnp.where(causal, g_i[:, 0:cl] - g_j, 0.0)), 0.0)
            k_c = k_ref[pl.ds(r0, cl), hs]
            kk = _bdot_nt(kb_ref[pl.ds(r0, cl), hs], k_c)
            s0 = pl.multiple_of((c * N_HEADS + h) * cl, cl)
            lmat_ref[pl.ds(s0, cl), :] = jnp.where(strict, kk * decay, 0.0)
            qk_ref[pl.ds(s0, cl), :] = _bdot_nt(q_ref[pl.ds(r0, cl), hs], k_c) * decay
        return carry

    lax.fori_loop(0, n_chunks, build, 0, unroll=2)

    for i in range(1, cl):
        lt_ref[i] = lmat_ref[pl.ds(i, n_sys, stride=cl), :].T
    nt_ref[...] = jnp.zeros(nt_ref.shape, F32)
    sub = 8
    for i in range(1, cl):
        acc = [-lt_ref[i, sub * r:sub * (r + 1), :] for r in range((i - 1) // sub + 1)]
        for j in range(1, i):
            l_ij = lt_ref[i, j:j + 1, :]
            for r in range((j - 1) // sub + 1):
                acc[r] = acc[r] - l_ij * nt_ref[j, sub * r:sub * (r + 1), :]
        for r, a in enumerate(acc):
            nt_ref[i, sub * r:sub * (r + 1), :] = a
    for i in range(cl):
        nmat_ref[pl.ds(i, n_sys, stride=cl), :] = nt_ref[i].T

    def solve(c, carry):
        r0 = pl.multiple_of(c * cl, cl)
        for h in range(N_HEADS):
            s0 = pl.multiple_of((c * N_HEADS + h) * cl, cl)
            cols = slice(2 * h * HEAD_DIM, (2 * h + 2) * HEAD_DIM)
            rhs = rhs_ref[pl.ds(r0, cl), cols]
            n_s = nmat_ref[pl.ds(s0, cl), :]
            n_lo = n_s - n_s.astype(BF16).astype(F32)
            r_lo = rhs - rhs.astype(BF16).astype(F32)
            rhs_ref[pl.ds(r0, cl), cols] = rhs + _bdot(jnp.concatenate([n_s, n_lo, n_s], axis=1),
                                                       jnp.concatenate([rhs, rhs, r_lo], axis=0))
        return carry

    lax.fori_loop(0, n_chunks, solve, 0, unroll=2)

    def step(c, carry):
        r0 = pl.multiple_of(c * cl, cl)
        for h in range(N_HEADS):
            hs = slice(h * HEAD_DIM, (h + 1) * HEAD_DIM)
            s0 = pl.multiple_of((c * N_HEADS + h) * cl, cl)
            s_old = state_ref[h]
            ws_qs = _bdot(jnp.concatenate(
                [rhs_ref[pl.ds(r0, cl), (2 * h + 1) * HEAD_DIM:(2 * h + 2) * HEAD_DIM],
                 qg_ref[pl.ds(r0, cl), hs]], axis=0), s_old)
            v_new = rhs_ref[pl.ds(r0, cl), 2 * h * HEAD_DIM:(2 * h + 1) * HEAD_DIM] - ws_qs[0:cl, :]
            g_i = gx_ref[pl.ds(r0, cl), hs]
            g_last = g_i[cl - 1:cl, :]
            k_dec = k_ref[pl.ds(r0, cl), hs] * jnp.exp(g_last - g_i)
            ov_sv = _bdot(jnp.concatenate([qk_ref[pl.ds(s0, cl), :], k_dec.T], axis=0), v_new)
            o_ref[pl.ds(r0, cl), hs] = ws_qs[cl:2 * cl, :] + ov_sv[0:cl, :]
            state_ref[h] = s_old * jnp.exp(g_last) + ov_sv[cl:, :]
        return carry

    lax.fori_loop(0, n_chunks, step, 0, unroll=2)

    yb_ref[...] = _bdot(_gated_out_norm(o_ref[...], z_ref[...], ng_ref[...]), wout_ref[...])

    @pl.when(t == pl.num_programs(1) - 1)
    def _():
        sbuf_ref[...] = ext_ref[tm + SHORT_TAIL - hist:tm + SHORT_TAIL, :]


def _delta_seq_call(qkv, z, ba, scw, alog, dtb, ng, wout, layer, tm):
    bsz, t_len, qkv_dim = qkv.shape
    v_dim = z.shape[-1]
    d = wout.shape[-1]
    hist = SHORT_KW - 1
    assert t_len % tm == 0 and tm % DELTA_CHUNK == 0
    cl = DELTA_CHUNK
    n_sys = (tm // cl) * N_HEADS

    def wspec(w):
        return _const_spec((None,) + w.shape[1:], lambda b, t: (layer, 0, 0))

    return pl.pallas_call(
        functools.partial(_delta_seq_kernel, tm=tm),
        grid=(bsz, t_len // tm),
        in_specs=[pl.BlockSpec((None, tm, qkv_dim), lambda b, t: (b, t, 0)),
                  pl.BlockSpec((None, tm, v_dim), lambda b, t: (b, t, 0)),
                  pl.BlockSpec((None, tm, LANES), lambda b, t: (b, t, 0)),
                  wspec(scw), wspec(alog), wspec(dtb), wspec(ng), wspec(wout)],
        out_specs=[pl.BlockSpec((None, tm, d), lambda b, t: (b, t, 0)),
                   pl.BlockSpec((None, hist, qkv_dim), lambda b, t: (b, 0, 0)),
                   pl.BlockSpec((None, N_HEADS, HEAD_DIM, HEAD_DIM), lambda b, t: (b, 0, 0, 0))],
        out_shape=[jax.ShapeDtypeStruct((bsz, t_len, d), F32),
                   jax.ShapeDtypeStruct((bsz, hist, qkv_dim), F32),
                   jax.ShapeDtypeStruct((bsz, N_HEADS, HEAD_DIM, HEAD_DIM), F32)],
        scratch_shapes=[pltpu.VMEM((tm + SHORT_TAIL, qkv_dim), F32),
                        pltpu.VMEM((tm, v_dim), F32),
                        pltpu.VMEM((tm, v_dim), F32),
                        pltpu.VMEM((tm, v_dim), F32),
                        pltpu.VMEM((tm, v_dim), F32),
                        pltpu.VMEM((tm, 2 * v_dim), F32),
                        pltpu.VMEM((tm, v_dim), F32),
                        pltpu.VMEM((tm, v_dim), F32),
                        pltpu.VMEM((n_sys * cl, cl), F32),
                        pltpu.VMEM((n_sys * cl, cl), F32),
                        pltpu.VMEM((cl, cl, n_sys), F32),
                        pltpu.VMEM((cl, cl, n_sys), F32),
                        pltpu.VMEM((n_sys * cl, cl), F32)],
        compiler_params=_params(2),
        name="delta_seq",
    )(qkv, z, ba, scw, alog, dtb, ng, wout)


def _delta_step_kernel(qkv_ref, z_ref, ba_ref, sst_ref, ds_ref, scw_ref, alog_ref, dtb_ref, ng_ref, wout_ref,
                       yb_ref, nsst_ref, nds_ref, o_ref):
    bb, qkv_dim = qkv_ref.shape
    hist = SHORT_KW - 1
    qk_dim = N_HEADS * HEAD_DIM
    x = qkv_ref[...]
    conv = x * scw_ref[hist:hist + 1, :]
    for j in range(hist):
        conv = conv + sst_ref[:, j * qkv_dim:(j + 1) * qkv_dim] * scw_ref[j:j + 1, :]
    nsst_ref[:, 0:(hist - 1) * qkv_dim] = sst_ref[:, qkv_dim:hist * qkv_dim]
    nsst_ref[:, (hist - 1) * qkv_dim:hist * qkv_dim] = x
    act = _silu(conv)
    q = _l2_normalize_heads(act[:, 0:qk_dim]) * (HEAD_DIM ** -0.5)
    k = _l2_normalize_heads(act[:, qk_dim:2 * qk_dim])
    v = act[:, 2 * qk_dim:]
    beta, g = _beta_and_log_decay(ba_ref[...], alog_ref[...], dtb_ref[...])
    beta_x = _expand_heads(beta, 0)
    eg_x = jnp.exp(_expand_heads(g, N_HEADS))
    kb = k * beta_x
    u_all = v * beta_x
    w_all = kb * eg_x
    qg = q * eg_x
    sub = lax.broadcasted_iota(jnp.int32, (8, HEAD_DIM), 0)
    for b in range(bb):
        for h in range(N_HEADS):
            hs = slice(h * HEAD_DIM, (h + 1) * HEAD_DIM)
            s_old = ds_ref[b, h]
            lhs = jnp.where(sub == 0, w_all[b:b + 1, hs], jnp.where(sub == 1, qg[b:b + 1, hs], 0.0))
            prod = _bdot(lhs, s_old)
            v_new = u_all[b:b + 1, hs] - prod[0:1, :]
            k_row = k[b:b + 1, hs]
            qk = jnp.sum(q[b:b + 1, hs] * k_row, axis=-1, keepdims=True)
            o_ref[b:b + 1, hs] = prod[1:2, :] + qk * v_new
            k_col = jnp.broadcast_to(k_row, (HEAD_DIM, HEAD_DIM)).T
            nds_ref[b, h] = s_old * eg_x[b:b + 1, hs] + k_col * v_new
    yb_ref[...] = _bdot(_gated_out_norm(o_ref[...], z_ref[...], ng_ref[...]), wout_ref[...])


def _delta_step_call(qkv, z, ba, sst, ds, scw, alog, dtb, ng, wout, layer, bb):
    bsz, qkv_dim = qkv.shape
    v_dim = z.shape[-1]
    d = wout.shape[-1]
    flat = sst.shape[-1]

    def wspec(w):
        return _const_spec((None,) + w.shape[1:], lambda i: (layer, 0, 0))

    state_block = (bb, N_HEADS, HEAD_DIM, HEAD_DIM)
    return pl.pallas_call(
        _delta_step_kernel,
        grid=(bsz // bb,),
        in_specs=[pl.BlockSpec((bb, qkv_dim), lambda i: (i, 0)),
                  pl.BlockSpec((bb, v_dim), lambda i: (i, 0)),
                  pl.BlockSpec((bb, LANES), lambda i: (i, 0)),
                  pl.BlockSpec((None, bb, flat), lambda i: (layer, i, 0)),
                  pl.BlockSpec((None,) + state_block, lambda i: (layer, i, 0, 0, 0)),
                  wspec(scw), wspec(alog), wspec(dtb), wspec(ng), wspec(wout)],
        out_specs=[pl.BlockSpec((bb, d), lambda i: (i, 0)),
                   pl.BlockSpec((bb, flat), lambda i: (i, 0)),
                   pl.BlockSpec(state_block, lambda i: (i, 0, 0, 0))],
        out_shape=[jax.ShapeDtypeStruct((bsz, d), F32),
                   jax.ShapeDtypeStruct((bsz, flat), F32),
                   jax.ShapeDtypeStruct((bsz, N_HEADS, HEAD_DIM, HEAD_DIM), F32)],
        scratch_shapes=[pltpu.VMEM((bb, v_dim), F32)],
        compiler_params=_params(1),
        name="delta_step",
    )(qkv, z, ba, sst, ds, scw, alog, dtb, ng, wout)


def _merge_ffn_kernel(x_ref, ya_ref, yb_ref, gate_ref, gt1_ref, sc2_ref, sh2_ref, gt2_ref, n2g_ref,
                      wm_ref, wfi_ref, wfo_ref, fin_ref, out_ref, *, ff_chunk, final_norm):
    d = x_ref.shape[-1]
    d_ff = wfo_ref.shape[0]
    merged = gate_ref[:, 0:d] * ya_ref[...] + gate_ref[:, d:2 * d] * yb_ref[...]
    x1 = x_ref[...] + gt1_ref[...] * _bdot(merged, wm_ref[...])
    h2 = x1 * lax.rsqrt(jnp.mean(x1 * x1, axis=-1, keepdims=True) + RMS_EPS) * n2g_ref[...]
    hb = (h2 * (1.0 + sc2_ref[...]) + sh2_ref[...]).astype(BF16)
    acc = jnp.zeros(x1.shape, F32)
    for c0 in range(0, d_ff, ff_chunk):
        gate = jnp.dot(hb, wfi_ref[:, c0:c0 + ff_chunk], preferred_element_type=F32)
        up = jnp.dot(hb, wfi_ref[:, d_ff + c0:d_ff + c0 + ff_chunk], preferred_element_type=F32)
        acc = acc + _bdot(_silu(gate) * up, wfo_ref[c0:c0 + ff_chunk, :])
    x2 = x1 + gt2_ref[...] * acc
    if final_norm:
        x2 = x2 * lax.rsqrt(jnp.mean(x2 * x2, axis=-1, keepdims=True) + RMS_EPS) * fin_ref[...]
    out_ref[...] = x2


def _merge_ffn_call(x, ya, yb, gates, mod, n2g, wm, wfi, wfo, fin_g, layer, tm, final_norm):
    groups, rows, d = x.shape
    d_ff = wfo.shape[1]
    ff_chunk = 256
    assert d_ff % ff_chunk == 0

    def wspec(w):
        return _const_spec((None,) + w.shape[1:], lambda g, i: (layer, 0, 0))

    def rowspec(w):
        return pl.BlockSpec((None, tm, w), lambda g, i: (g, i, 0))

    return pl.pallas_call(
        functools.partial(_merge_ffn_kernel, ff_chunk=ff_chunk, final_norm=final_norm),
        grid=(groups, rows // tm),
        in_specs=[rowspec(d), rowspec(d), rowspec(d), rowspec(2 * d),
                  _mod_spec(mod, tm, 2), _mod_spec(mod, tm, 4), _mod_spec(mod, tm, 3), _mod_spec(mod, tm, 5),
                  wspec(n2g), wspec(wm), wspec(wfi), wspec(wfo),
                  _const_spec((1, d), lambda g, i: (0, 0))],
        out_specs=rowspec(d),
        out_shape=jax.ShapeDtypeStruct((groups, rows, d), F32),
        compiler_params=_params(2),
        name="merge_ffn",
    )(x, ya, yb, gates, mod, mod, mod, mod, n2g, wm, wfi, wfo, fin_g)


def kernel(x_prompt, x_sample, c_prompt, c_sample, state_conformer_conv, state_short_conv, state_delta, w_ada, b_ada, norm1_g, w_in, conf_dw_w, conf_dw_b, conf_ln_g, conf_ln_b, w_conf_out, short_conv_w, a_log, dt_bias, delta_norm_g, w_delta_out, w_merge_out, norm2_g, w_ffn_in, w_ffn_out, final_norm_g):
    depth, d, in_dim = w_in.shape
    bp, t_len, _ = x_prompt.shape
    bs = x_sample.shape[0]
    assert x_sample.shape[1] == 1
    conf_ch = conf_dw_w.shape[-1]
    qkv_dim = short_conv_w.shape[-1]
    v_dim = w_delta_out.shape[1]
    assert v_dim == N_HEADS * HEAD_DIM and qkv_dim == 3 * v_dim
    assert conf_dw_w.shape[1] == CONF_KW and short_conv_w.shape[1] == SHORT_KW

    o_glu = 2 * conf_ch
    o_qkv = o_glu + qkv_dim
    o_z = o_qkv + v_dim
    o_ba = o_z + 2 * N_HEADS
    assert in_dim == o_ba + 2 * d
    wglu = w_in[:, :, :o_glu].astype(BF16)
    wqkv = w_in[:, :, o_glu:o_qkv].astype(BF16)
    wz = w_in[:, :, o_qkv:o_z].astype(BF16)
    wba = jnp.pad(w_in[:, :, o_z:o_ba].astype(BF16), ((0, 0), (0, 0), (0, LANES - 2 * N_HEADS)))
    wmg = w_in[:, :, o_ba:].astype(BF16)
    w_conf_out_b = w_conf_out.astype(BF16)
    w_delta_out_b = w_delta_out.astype(BF16)
    w_merge_b = w_merge_out.astype(BF16)
    w_ffn_in_b = w_ffn_in.astype(BF16)
    w_ffn_out_b = w_ffn_out.astype(BF16)

    def row3(a):
        return a.reshape(depth, 1, a.shape[-1])

    norm1 = row3(norm1_g)
    norm2 = row3(norm2_g)
    dw_b = row3(conf_dw_b)
    ln_g = row3(conf_ln_g)
    ln_b = row3(conf_ln_b)
    alog = jnp.pad(a_log, ((0, 0), (N_HEADS, LANES - 2 * N_HEADS))).reshape(depth, 1, LANES)
    dtb = jnp.pad(dt_bias, ((0, 0), (N_HEADS, LANES - 2 * N_HEADS))).reshape(depth, 1, LANES)
    ng = row3(jnp.tile(delta_norm_g, (1, N_HEADS)))
    fin_g = final_norm_g.reshape(1, d)

    mod = _ada_call(jnp.concatenate([c_prompt, c_sample], axis=0), w_ada, b_ada)
    mod_p = mod[:, :bp].reshape(depth, bp, 1, 6 * d)
    mod_s = mod[:, bp:].reshape(depth, 1, bs, 6 * d)
    conf_state = state_conformer_conv.reshape(depth, bs, (CONF_KW - 1) * conf_ch)
    short_state = state_short_conv.reshape(depth, bs, (SHORT_KW - 1) * qkv_dim)

    xp = x_prompt
    xs = x_sample.reshape(1, bs, d)
    conf_p, conf_s, short_p, short_s, delta_p, delta_s = [], [], [], [], [], []
    for l in range(depth):
        last = l == depth - 1
        u, qkv, z, ba, gates = _inproj_call(xp, mod_p[l], norm1, wglu, wqkv, wz, wba, wmg, l, tm=512)
        ya, cb = _conf_seq_call(u, conf_dw_w, dw_b, ln_g, ln_b, w_conf_out_b, l, tm=256)
        yb, sb, ds = _delta_seq_call(qkv, z, ba, short_conv_w, alog, dtb, ng, w_delta_out_b, l, tm=512)
        xp = _merge_ffn_call(xp, ya, yb, gates, mod_p[l], norm2, w_merge_b, w_ffn_in_b, w_ffn_out_b, fin_g,
                             l, tm=512, final_norm=last)
        conf_p.append(cb)
        short_p.append(sb)
        delta_p.append(ds)
        u, qkv, z, ba, gates = _inproj_call(xs, mod_s[l], norm1, wglu, wqkv, wz, wba, wmg, l, tm=bs)
        ya, cb = _conf_step_call(u[0], conf_state, conf_dw_w, dw_b, ln_g, ln_b, w_conf_out_b, l, bb=32)
        yb, sb, ds = _delta_step_call(qkv[0], z[0], ba[0], short_state, state_delta, short_conv_w, alog, dtb, ng,
                                      w_delta_out_b, l, bb=8)
        xs = _merge_ffn_call(xs, ya[None], yb[None], gates, mod_s[l], norm2, w_merge_b, w_ffn_in_b, w_ffn_out_b,
                             fin_g, l, tm=bs, final_norm=last)
        conf_s.append(cb.reshape(bs, CONF_KW - 1, conf_ch))
        short_s.append(sb.reshape(bs, SHORT_KW - 1, qkv_dim))
        delta_s.append(ds)
    return (xp, xs.reshape(bs, 1, d), jnp.stack(conf_p), jnp.stack(conf_s), jnp.stack(short_p),
            jnp.stack(short_s), jnp.stack(delta_p), jnp.stack(delta_s))
```

```python
import functools

import jax
import jax.numpy as jnp
from jax import lax
from jax.experimental import pallas as pl
from jax.experimental.pallas import tpu as pltpu

F32 = jnp.float32
BF16 = jnp.bfloat16

CONF_KW = 31
SHORT_KW = 4
N_HEADS = 4
HEAD_DIM = 128
RMS_EPS = 1e-6
LN_EPS = 1e-5
L2_EPS = 1e-6
DELTA_CHUNK = 64
LANES = 128
CONF_TAIL = 32
SHORT_TAIL = 8
VMEM_LIMIT = 56 * 1024 * 1024


def _bdot(a, b):
    return jnp.dot(a.astype(BF16), b.astype(BF16), preferred_element_type=F32)


def _bdot_nt(a, b):
    return lax.dot_general(a.astype(BF16), b.astype(BF16), (((1,), (1,)), ((), ())),
                           preferred_element_type=F32)


def _silu(x):
    return x * jax.nn.sigmoid(x)


def _softplus(x):
    return jnp.maximum(x, 0.0) + jnp.log1p(jnp.exp(-jnp.abs(x)))


def _const_spec(shape, index_map):
    return pl.BlockSpec(shape, index_map, pipeline_mode=pl.Buffered(1))


def _params(n_axes):
    return pltpu.CompilerParams(dimension_semantics=("arbitrary",) * n_axes,
                                vmem_limit_bytes=VMEM_LIMIT)


def _cast_kernel(w_ref, o_ref):
    o_ref[...] = w_ref[...].astype(BF16)


def _cast_call(w, rows_per_block):
    depth, rows, cols = w.shape
    assert rows % rows_per_block == 0
    spec = pl.BlockSpec((None, rows_per_block, cols), lambda l, i: (l, i, 0))
    return pl.pallas_call(
        _cast_kernel,
        grid=(depth, rows // rows_per_block),
        in_specs=[spec],
        out_specs=spec,
        out_shape=jax.ShapeDtypeStruct(w.shape, BF16),
        compiler_params=_params(2),
        name="cast_weights",
    )(w)


def _split_w_in_kernel(w_ref, glu_ref, ba_ref, cat_ref, *, o_glu, o_z, o_ba, n_plain):
    cw = cat_ref.shape[-1]
    n_gate = cat_ref.shape[0] - n_plain
    glu_ref[...] = w_ref[:, 0:o_glu].astype(BF16)
    for c in range(n_plain):
        cat_ref[c] = w_ref[:, o_glu + c * cw:o_glu + (c + 1) * cw].astype(BF16)
    tail = w_ref[:, o_z:o_ba + n_gate * cw]
    lane = lax.broadcasted_iota(jnp.int32, (tail.shape[0], LANES), 1)
    ba_ref[...] = jnp.where(lane < o_ba - o_z, tail[:, 0:LANES], 0.0).astype(BF16)
    off = o_ba - o_z
    for c in range(n_gate):
        cat_ref[n_plain + c] = tail[:, off + c * cw:off + (c + 1) * cw].astype(BF16)


def _split_w_in_call(w_in, o_glu, o_z, o_ba, cw, rows_per_block):
    depth, d, in_dim = w_in.shape
    n_plain = (o_z - o_glu) // cw
    n_gate = (in_dim - o_ba) // cw
    assert o_z % LANES == 0 and o_ba + n_gate * cw == in_dim and 0 < o_ba - o_z <= LANES
    assert d % rows_per_block == 0
    return pl.pallas_call(
        functools.partial(_split_w_in_kernel, o_glu=o_glu, o_z=o_z, o_ba=o_ba, n_plain=n_plain),
        grid=(depth, d // rows_per_block),
        in_specs=[pl.BlockSpec((None, rows_per_block, in_dim), lambda l, i: (l, i, 0))],
        out_specs=[pl.BlockSpec((None, rows_per_block, o_glu), lambda l, i: (l, i, 0)),
                   pl.BlockSpec((None, rows_per_block, LANES), lambda l, i: (l, i, 0)),
                   pl.BlockSpec((None, n_plain + n_gate, rows_per_block, cw), lambda l, i: (l, 0, i, 0))],
        out_shape=[jax.ShapeDtypeStruct((depth, d, o_glu), BF16),
                   jax.ShapeDtypeStruct((depth, d, LANES), BF16),
                   jax.ShapeDtypeStruct((depth, n_plain + n_gate, d, cw), BF16)],
        compiler_params=_params(2),
        name="split_w_in",
    )(w_in)


def _ada_kernel(c_ref, w_ref, b_ref, o_ref):
    c = c_ref[...]
    o_ref[...] = _bdot(_silu(c), w_ref[...]) + b_ref[...]


def _ada_call(c_all, w_ada, b_ada):
    depth, d, six_d = w_ada.shape
    rows = c_all.shape[0]
    tn = 1536
    return pl.pallas_call(
        _ada_kernel,
        grid=(depth, six_d // tn),
        in_specs=[
            pl.BlockSpec((rows, d), lambda l, j: (0, 0)),
            pl.BlockSpec((None, d, tn), lambda l, j: (l, 0, j)),
            pl.BlockSpec((None, 1, tn), lambda l, j: (l, 0, j)),
        ],
        out_specs=pl.BlockSpec((None, rows, tn), lambda l, j: (l, 0, j)),
        out_shape=jax.ShapeDtypeStruct((depth, rows, six_d), F32),
        compiler_params=_params(2),
        name="ada_mod",
    )(c_all, w_ada, b_ada.reshape(depth, 1, six_d))


def _modulated_norm(x_ref, sc_ref, sh_ref, g_ref):
    x = x_ref[...]
    h = x * lax.rsqrt(jnp.mean(x * x, axis=-1, keepdims=True) + RMS_EPS) * g_ref[...]
    return (h * (1.0 + sc_ref[...]) + sh_ref[...]).astype(BF16)


def _inproj_kernel(x_ref, sc_ref, sh_ref, g_ref, wglu_ref, wcat_ref, wba_ref,
                   u_ref, proj_ref, ba_ref, *, conf_ch, n_plain):
    hb = _modulated_norm(x_ref, sc_ref, sh_ref, g_ref)
    glu = jnp.dot(hb, wglu_ref[...], preferred_element_type=F32)
    u_ref[...] = glu[:, :conf_ch] * jax.nn.sigmoid(glu[:, conf_ch:])
    ba_ref[...] = jnp.dot(hb, wba_ref[...], preferred_element_type=F32)
    for i in range(wcat_ref.shape[0]):
        chunk = jnp.dot(hb, wcat_ref[i], preferred_element_type=F32)
        proj_ref[i] = chunk if i < n_plain else jax.nn.sigmoid(chunk)


def _layer_norm_swish(ca, g, b):
    mu = jnp.mean(ca, axis=-1, keepdims=True)
    cen = ca - mu
    var = jnp.mean(cen * cen, axis=-1, keepdims=True)
    return _silu(cen * lax.rsqrt(var + LN_EPS) * g + b)


def _inproj_conf_kernel(x_ref, sc_ref, sh_ref, g_ref, wglu_ref, wcat_ref, wba_ref,
                        dww_ref, dwb_ref, lng_ref, lnb_ref, wco_ref,
                        ya_ref, proj_ref, ba_ref, buf_ref,
                        hb_ref, ext_ref, ca_ref, pr_ref, *, conf_ch, tm, rc, n_plain):
    t = pl.program_id(1)
    hist = CONF_KW - 1
    n_chunks = wcat_ref.shape[0]
    assert n_chunks * rc == tm

    @pl.when(t == 0)
    def _():
        ext_ref[0:CONF_TAIL, :] = jnp.zeros((CONF_TAIL, conf_ch), F32)

    @pl.when(t > 0)
    def _():
        ext_ref[0:CONF_TAIL, :] = ext_ref[tm:tm + CONF_TAIL, :]

    hb_ref[...] = _modulated_norm(x_ref, sc_ref, sh_ref, g_ref)
    glu = jnp.dot(hb_ref[...], wglu_ref[...], preferred_element_type=F32)
    ext_ref[CONF_TAIL:CONF_TAIL + tm, :] = glu[:, :conf_ch] * jax.nn.sigmoid(glu[:, conf_ch:])
    ba_ref[...] = jnp.dot(hb_ref[...], wba_ref[...], preferred_element_type=F32)

    sub = 8
    base = CONF_TAIL - hist

    def conv_rows(i):
        r0 = pl.multiple_of(i * rc, rc)
        for r in range(sub):
            rows = rc + (sub if r else 0)
            acc = None
            for o in range(r if r >= base else r + sub, base + CONF_KW, sub):
                start = pl.multiple_of(r0 + (o - r), sub)
                term = ext_ref[pl.ds(start, rows), :] * dww_ref[o - base:o - base + 1, :]
                acc = term if acc is None else acc + term
            pr_ref[r, 0:rows, :] = acc
        y = pr_ref[0, 0:rc, :]
        for r in range(1, sub):
            y = y + pr_ref[r, r:r + rc, :]
        ca_ref[pl.ds(r0, rc), :] = y

    def plain_chunk(i, carry):
        proj_ref[i] = jnp.dot(hb_ref[...], wcat_ref[i], preferred_element_type=F32)
        conv_rows(i)
        return carry

    def gated_chunk(i, carry):
        proj_ref[i] = jax.nn.sigmoid(jnp.dot(hb_ref[...], wcat_ref[i], preferred_element_type=F32))
        conv_rows(i)
        return carry

    lax.fori_loop(0, n_plain, plain_chunk, 0, unroll=2)
    lax.fori_loop(n_plain, n_chunks, gated_chunk, 0, unroll=2)
    act = _layer_norm_swish(ca_ref[...] + dwb_ref[...], lng_ref[...], lnb_ref[...])
    ya_ref[...] = _bdot(act, wco_ref[...])

    @pl.when(t == pl.num_programs(1) - 1)
    def _():
        buf_ref[...] = ext_ref[tm + CONF_TAIL - hist:tm + CONF_TAIL, :]


def _mod_spec(mod, tm, col):
    d = mod.shape[-1] // 6
    if mod.shape[1] == 1:
        return pl.BlockSpec((None, 1, d), lambda g, i: (g, 0, col))
    return pl.BlockSpec((None, tm, d), lambda g, i: (g, i, col))


def _inproj_call(x, mod, norm_g, wglu, wcat, wba, layer, tm, n_plain):
    groups, rows, d = x.shape
    conf_ch = wglu.shape[-1] // 2
    n_chunks, cw = wcat.shape[1], wcat.shape[3]

    def wspec(w):
        return _const_spec((None,) + w.shape[1:], lambda g, i: (layer,) + (0,) * (w.ndim - 1))

    return pl.pallas_call(
        functools.partial(_inproj_kernel, conf_ch=conf_ch, n_plain=n_plain),
        grid=(groups, rows // tm),
        in_specs=[
            pl.BlockSpec((None, tm, d), lambda g, i: (g, i, 0)),
            _mod_spec(mod, tm, 1),
            _mod_spec(mod, tm, 0),
            wspec(norm_g), wspec(wglu), wspec(wcat), wspec(wba),
        ],
        out_specs=[pl.BlockSpec((None, tm, conf_ch), lambda g, i: (g, i, 0)),
                   pl.BlockSpec((None, n_chunks, tm, cw), lambda g, i: (g, 0, i, 0)),
                   pl.BlockSpec((None, tm, wba.shape[-1]), lambda g, i: (g, i, 0))],
        out_shape=[jax.ShapeDtypeStruct((groups, rows, conf_ch), F32),
                   jax.ShapeDtypeStruct((groups, n_chunks, rows, cw), F32),
                   jax.ShapeDtypeStruct((groups, rows, wba.shape[-1]), F32)],
        compiler_params=_params(2),
        name="in_proj",
    )(x, mod, mod, norm_g, wglu, wcat, wba)


def _inproj_conf_call(x, mod, norm_g, wglu, wcat, wba, dw_w, dw_b, ln_g, ln_b, wco, layer, tm, n_plain):
    bsz, t_len, d = x.shape
    conf_ch = wglu.shape[-1] // 2
    hist = CONF_KW - 1
    n_chunks, cw = wcat.shape[1], wcat.shape[3]
    rc = tm // n_chunks
    assert t_len % tm == 0 and tm % n_chunks == 0 and rc % 8 == 0 and t_len >= hist

    def wspec(w):
        return _const_spec((None,) + w.shape[1:], lambda g, i: (layer,) + (0,) * (w.ndim - 1))

    return pl.pallas_call(
        functools.partial(_inproj_conf_kernel, conf_ch=conf_ch, tm=tm, rc=rc, n_plain=n_plain),
        grid=(bsz, t_len // tm),
        in_specs=[
            pl.BlockSpec((None, tm, d), lambda g, i: (g, i, 0)),
            _mod_spec(mod, tm, 1),
            _mod_spec(mod, tm, 0),
            wspec(norm_g), wspec(wglu), wspec(wcat), wspec(wba),
            wspec(dw_w), wspec(dw_b), wspec(ln_g), wspec(ln_b), wspec(wco),
        ],
        out_specs=[pl.BlockSpec((None, tm, d), lambda g, i: (g, i, 0)),
                   pl.BlockSpec((None, n_chunks, tm, cw), lambda g, i: (g, 0, i, 0)),
                   pl.BlockSpec((None, tm, wba.shape[-1]), lambda g, i: (g, i, 0)),
                   pl.BlockSpec((None, hist, conf_ch), lambda g, i: (g, 0, 0))],
        out_shape=[jax.ShapeDtypeStruct((bsz, t_len, d), F32),
                   jax.ShapeDtypeStruct((bsz, n_chunks, t_len, cw), F32),
                   jax.ShapeDtypeStruct((bsz, t_len, wba.shape[-1]), F32),
                   jax.ShapeDtypeStruct((bsz, hist, conf_ch), F32)],
        scratch_shapes=[pltpu.VMEM((tm, d), BF16),
                        pltpu.VMEM((tm + CONF_TAIL, conf_ch), F32), pltpu.VMEM((tm, conf_ch), F32),
                        pltpu.VMEM((8, rc + 8, conf_ch), F32)],
        compiler_params=_params(2),
        name="in_proj_conformer",
    )(x, mod, mod, norm_g, wglu, wcat, wba, dw_w, dw_b, ln_g, ln_b, wco)


def _conf_step_kernel(u_ref, st_ref, w_ref, b_ref, lng_ref, lnb_ref, wout_ref, ya_ref, nst_ref):
    ch = u_ref.shape[1]
    hist = CONF_KW - 1
    u = u_ref[...]
    acc = u * w_ref[hist:hist + 1, :]
    for j in range(hist):
        acc = acc + st_ref[:, j * ch:(j + 1) * ch] * w_ref[j:j + 1, :]
    act = _layer_norm_swish(acc + b_ref[...], lng_ref[...], lnb_ref[...])
    ya_ref[...] = _bdot(act, wout_ref[...])
    nst_ref[:, 0:(hist - 1) * ch] = st_ref[:, ch:hist * ch]
    nst_ref[:, (hist - 1) * ch:hist * ch] = u


def _conf_step_call(u, st, dw_w, dw_b, ln_g, ln_b, wout, layer, bb):
    bsz, ch = u.shape
    d = wout.shape[-1]
    flat = st.shape[-1]

    def wspec(w):
        return _const_spec((None,) + w.shape[1:], lambda i: (layer, 0, 0))

    return pl.pallas_call(
        _conf_step_kernel,
        grid=(bsz // bb,),
        in_specs=[pl.BlockSpec((bb, ch), lambda i: (i, 0)),
                  pl.BlockSpec((None, bb, flat), lambda i: (layer, i, 0)),
                  wspec(dw_w), wspec(dw_b), wspec(ln_g), wspec(ln_b), wspec(wout)],
        out_specs=[pl.BlockSpec((bb, d), lambda i: (i, 0)),
                   pl.BlockSpec((bb, flat), lambda i: (i, 0))],
        out_shape=[jax.ShapeDtypeStruct((bsz, d), F32),
                   jax.ShapeDtypeStruct((bsz, flat), F32)],
        compiler_params=_params(1),
        name="conformer_step",
    )(u, st, dw_w, dw_b, ln_g, ln_b, wout)


def _head_column(x, lane):
    ids = lax.broadcasted_iota(jnp.int32, x.shape, 1)
    return jnp.sum(jnp.where(ids == lane, x, 0.0), axis=-1, keepdims=True)


def _expand_heads(x, first_lane):
    rows = x.shape[0]
    return jnp.concatenate(
        [jnp.broadcast_to(_head_column(x, first_lane + h), (rows, HEAD_DIM)) for h in range(N_HEADS)], axis=1)


def _l2_normalize_heads(x):
    outs = []
    for h in range(N_HEADS):
        xh = x[:, h * HEAD_DIM:(h + 1) * HEAD_DIM]
        outs.append(xh * lax.rsqrt(jnp.sum(xh * xh, axis=-1, keepdims=True) + L2_EPS))
    return jnp.concatenate(outs, axis=1)


def _gated_out_norm(o, z, ng):
    outs = []
    for h in range(N_HEADS):
        oh = o[:, h * HEAD_DIM:(h + 1) * HEAD_DIM]
        outs.append(oh * lax.rsqrt(jnp.mean(oh * oh, axis=-1, keepdims=True) + RMS_EPS))
    return jnp.concatenate(outs, axis=1) * ng * _silu(z)


def _beta_and_log_decay(ba, alog, dtb):
    beta = jax.nn.sigmoid(ba)
    g = -jnp.exp(alog) * _softplus(ba + dtb)
    return beta, g


def _delta_seq_kernel(qin_ref, kin_ref, vin_ref, z_ref, ba_ref, scw_ref, alog_ref, dtb_ref, ng_ref, wout_ref,
                      yb_ref, sbuf_ref, state_ref,
                      ext_ref, o_ref, q_ref, k_ref, kb_ref, rhs_ref, qg_ref, gx_ref,
                      lmat_ref, qk_ref, lt_ref, nt_ref, nmat_ref, *, tm):
    t = pl.program_id(1)
    hist = SHORT_KW - 1
    qk_dim = N_HEADS * HEAD_DIM
    cl = DELTA_CHUNK
    n_chunks = tm // cl
    n_sys = n_chunks * N_HEADS

    @pl.when(t == 0)
    def _():
        ext_ref[0:SHORT_TAIL, :] = jnp.zeros((SHORT_TAIL, ext_ref.shape[1]), F32)
        state_ref[...] = jnp.zeros(state_ref.shape, F32)

    @pl.when(t > 0)
    def _():
        ext_ref[0:SHORT_TAIL, :] = ext_ref[tm:tm + SHORT_TAIL, :]

    ext_ref[SHORT_TAIL:SHORT_TAIL + tm, 0:qk_dim] = qin_ref[...]
    ext_ref[SHORT_TAIL:SHORT_TAIL + tm, qk_dim:2 * qk_dim] = kin_ref[...]
    ext_ref[SHORT_TAIL:SHORT_TAIL + tm, 2 * qk_dim:] = vin_ref[...]
    conv = ext_ref[SHORT_TAIL - hist:SHORT_TAIL - hist + tm, :] * scw_ref[0:1, :]
    for j in range(1, SHORT_KW):
        conv = conv + ext_ref[SHORT_TAIL - hist + j:SHORT_TAIL - hist + j + tm, :] * scw_ref[j:j + 1, :]
    act = _silu(conv)
    q = _l2_normalize_heads(act[:, 0:qk_dim]) * (HEAD_DIM ** -0.5)
    k = _l2_normalize_heads(act[:, qk_dim:2 * qk_dim])
    v = act[:, 2 * qk_dim:]

    beta, g = _beta_and_log_decay(ba_ref[...], alog_ref[...], dtb_ref[...])
    pos = lax.broadcasted_iota(jnp.int32, g.shape, 0) % cl
    shift = 1
    while shift < cl:
        g = g + jnp.where(pos >= shift, pltpu.roll(g, shift, 0), 0.0)
        shift *= 2
    beta_x = _expand_heads(beta, 0)
    g_x = _expand_heads(g, N_HEADS)
    eg_x = jnp.exp(g_x)
    kb = k * beta_x
    q_ref[...] = q
    k_ref[...] = k
    kb_ref[...] = kb
    qg_ref[...] = q * eg_x
    gx_ref[...] = g_x
    rhs_u = v * beta_x
    rhs_w = kb * eg_x
    for h in range(N_HEADS):
        hs = slice(h * HEAD_DIM, (h + 1) * HEAD_DIM)
        rhs_ref[:, 2 * h * HEAD_DIM:(2 * h + 1) * HEAD_DIM] = rhs_u[:, hs]
        rhs_ref[:, (2 * h + 1) * HEAD_DIM:(2 * h + 2) * HEAD_DIM] = rhs_w[:, hs]

    row = lax.broadcasted_iota(jnp.int32, (cl, cl), 0)
    col = lax.broadcasted_iota(jnp.int32, (cl, cl), 1)
    causal = row >= col
    strict = row > col

    def build(c, carry):
        r0 = pl.multiple_of(c * cl, cl)
        for h in range(N_HEADS):
            hs = slice(h * HEAD_DIM, (h + 1) * HEAD_DIM)
            g_i = gx_ref[pl.ds(r0, cl), hs]
            g_j = g_i.T[0:cl, :]
            decay = jnp.where(causal, jnp.exp(jnp.where(causal, g_i[:, 0:cl] - g_j, 0.0)), 0.0)
            k_c = k_ref[pl.ds(r0, cl), hs]
            kk = _bdot_nt(kb_ref[pl.ds(r0, cl), hs], k_c)
            s0 = pl.multiple_of((c * N_HEADS + h) * cl, cl)
            lmat_ref[pl.ds(s0, cl), :] = jnp.where(strict, kk * decay, 0.0)
            qk_ref[pl.ds(s0, cl), :] = _bdot_nt(q_ref[pl.ds(r0, cl), hs], k_c) * decay
        return carry

    lax.fori_loop(0, n_chunks, build, 0, unroll=2)

    for i in range(1, cl):
        lt_ref[i] = lmat_ref[pl.ds(i, n_sys, stride=cl), :].T
    nt_ref[...] = jnp.zeros(nt_ref.shape, F32)
    sub = 8
    for i in range(1, cl):
        acc = [-lt_ref[i, sub * r:sub * (r + 1), :] for r in range((i - 1) // sub + 1)]
        for j in range(1, i):
            l_ij = lt_ref[i, j:j + 1, :]
            for r in range((j - 1) // sub + 1):
                acc[r] = acc[r] - l_ij * nt_ref[j, sub * r:sub * (r + 1), :]
        for r, a in enumerate(acc):
            nt_ref[i, sub * r:sub * (r + 1), :] = a
    for i in range(cl):
        nmat_ref[pl.ds(i, n_sys, stride=cl), :] = nt_ref[i].T

    def solve(c, carry):
        r0 = pl.multiple_of(c * cl, cl)
        for h in range(N_HEADS):
            s0 = pl.multiple_of((c * N_HEADS + h) * cl, cl)
            cols = slice(2 * h * HEAD_DIM, (2 * h + 2) * HEAD_DIM)
            rhs = rhs_ref[pl.ds(r0, cl), cols]
            n_s = nmat_ref[pl.ds(s0, cl), :]
            n_lo = n_s - n_s.astype(BF16).astype(F32)
            r_lo = rhs - rhs.astype(BF16).astype(F32)
            rhs_ref[pl.ds(r0, cl), cols] = rhs + _bdot(jnp.concatenate([n_s, n_lo, n_s], axis=1),
                                                       jnp.concatenate([rhs, rhs, r_lo], axis=0))
        return carry

    lax.fori_loop(0, n_chunks, solve, 0, unroll=2)

    def step(c, carry):
        r0 = pl.multiple_of(c * cl, cl)
        for h in range(N_HEADS):
            hs = slice(h * HEAD_DIM, (h + 1) * HEAD_DIM)
            s0 = pl.multiple_of((c * N_HEADS + h) * cl, cl)
            s_old = state_ref[h]
            ws_qs = _bdot(jnp.concatenate(
                [rhs_ref[pl.ds(r0, cl), (2 * h + 1) * HEAD_DIM:(2 * h + 2) * HEAD_DIM],
                 qg_ref[pl.ds(r0, cl), hs]], axis=0), s_old)
            v_new = rhs_ref[pl.ds(r0, cl), 2 * h * HEAD_DIM:(2 * h + 1) * HEAD_DIM] - ws_qs[0:cl, :]
            g_i = gx_ref[pl.ds(r0, cl), hs]
            g_last = g_i[cl - 1:cl, :]
            k_dec = k_ref[pl.ds(r0, cl), hs] * jnp.exp(g_last - g_i)
            ov_sv = _bdot(jnp.concatenate([qk_ref[pl.ds(s0, cl), :], k_dec.T], axis=0), v_new)
            o_ref[pl.ds(r0, cl), hs] = ws_qs[cl:2 * cl, :] + ov_sv[0:cl, :]
            state_ref[h] = s_old * jnp.exp(g_last) + ov_sv[cl:, :]
        return carry

    lax.fori_loop(0, n_chunks, step, 0, unroll=2)

    yb_ref[...] = _bdot(_gated_out_norm(o_ref[...], z_ref[...], ng_ref[...]), wout_ref[...])

    @pl.when(t == pl.num_programs(1) - 1)
    def _():
        sbuf_ref[...] = ext_ref[tm + SHORT_TAIL - hist:tm + SHORT_TAIL, :]


def _delta_seq_call(proj, ba, scw, alog, dtb, ng, wout, layer, tm):
    bsz, _, t_len, v_dim = proj.shape
    qkv_dim = 3 * v_dim
    d = wout.shape[-1]
    hist = SHORT_KW - 1
    assert t_len % tm == 0 and tm % DELTA_CHUNK == 0
    cl = DELTA_CHUNK
    n_sys = (tm // cl) * N_HEADS

    def wspec(w):
        return _const_spec((None,) + w.shape[1:], lambda b, t: (layer, 0, 0))

    def chunk_spec(c):
        return pl.BlockSpec((None, None, tm, v_dim), lambda b, t: (b, c, t, 0))

    return pl.pallas_call(
        functools.partial(_delta_seq_kernel, tm=tm),
        grid=(bsz, t_len // tm),
        in_specs=[chunk_spec(0), chunk_spec(1), chunk_spec(2), chunk_spec(3),
                  pl.BlockSpec((None, tm, LANES), lambda b, t: (b, t, 0)),
                  wspec(scw), wspec(alog), wspec(dtb), wspec(ng), wspec(wout)],
        out_specs=[pl.BlockSpec((None, tm, d), lambda b, t: (b, t, 0)),
                   pl.BlockSpec((None, hist, qkv_dim), lambda b, t: (b, 0, 0)),
                   pl.BlockSpec((None, N_HEADS, HEAD_DIM, HEAD_DIM), lambda b, t: (b, 0, 0, 0))],
        out_shape=[jax.ShapeDtypeStruct((bsz, t_len, d), F32),
                   jax.ShapeDtypeStruct((bsz, hist, qkv_dim), F32),
                   jax.ShapeDtypeStruct((bsz, N_HEADS, HEAD_DIM, HEAD_DIM), F32)],
        scratch_shapes=[pltpu.VMEM((tm + SHORT_TAIL, qkv_dim), F32),
                        pltpu.VMEM((tm, v_dim), F32),
                        pltpu.VMEM((tm, v_dim), F32),
                        pltpu.VMEM((tm, v_dim), F32),
                        pltpu.VMEM((tm, v_dim), F32),
                        pltpu.VMEM((tm, 2 * v_dim), F32),
                        pltpu.VMEM((tm, v_dim), F32),
                        pltpu.VMEM((tm, v_dim), F32),
                        pltpu.VMEM((n_sys * cl, cl), F32),
                        pltpu.VMEM((n_sys * cl, cl), F32),
                        pltpu.VMEM((cl, cl, n_sys), F32),
                        pltpu.VMEM((cl, cl, n_sys), F32),
                        pltpu.VMEM((n_sys * cl, cl), F32)],
        compiler_params=_params(2),
        name="delta_seq",
    )(proj, proj, proj, proj, ba, scw, alog, dtb, ng, wout)


def _delta_step_kernel(qkv_ref, z_ref, ba_ref, sst_ref, ds_ref, scw_ref, alog_ref, dtb_ref, ng_ref, wout_ref,
                       yb_ref, nsst_ref, nds_ref, o_ref):
    bb, qkv_dim = qkv_ref.shape
    hist = SHORT_KW - 1
    qk_dim = N_HEADS * HEAD_DIM
    x = qkv_ref[...]
    conv = x * scw_ref[hist:hist + 1, :]
    for j in range(hist):
        conv = conv + sst_ref[:, j * qkv_dim:(j + 1) * qkv_dim] * scw_ref[j:j + 1, :]
    nsst_ref[:, 0:(hist - 1) * qkv_dim] = sst_ref[:, qkv_dim:hist * qkv_dim]
    nsst_ref[:, (hist - 1) * qkv_dim:hist * qkv_dim] = x
    act = _silu(conv)
    q = _l2_normalize_heads(act[:, 0:qk_dim]) * (HEAD_DIM ** -0.5)
    k = _l2_normalize_heads(act[:, qk_dim:2 * qk_dim])
    v = act[:, 2 * qk_dim:]
    beta, g = _beta_and_log_decay(ba_ref[...], alog_ref[...], dtb_ref[...])
    beta_x = _expand_heads(beta, 0)
    eg_x = jnp.exp(_expand_heads(g, N_HEADS))
    kb = k * beta_x
    u_all = v * beta_x
    w_all = kb * eg_x
    qg = q * eg_x
    sub = lax.broadcasted_iota(jnp.int32, (8, HEAD_DIM), 0)
    for b in range(bb):
        for h in range(N_HEADS):
            hs = slice(h * HEAD_DIM, (h + 1) * HEAD_DIM)
            s_old = ds_ref[b, h]
            lhs = jnp.where(sub == 0, w_all[b:b + 1, hs], jnp.where(sub == 1, qg[b:b + 1, hs], 0.0))
            prod = _bdot(lhs, s_old)
            v_new = u_all[b:b + 1, hs] - prod[0:1, :]
            k_row = k[b:b + 1, hs]
            qk = jnp.sum(q[b:b + 1, hs] * k_row, axis=-1, keepdims=True)
            o_ref[b:b + 1, hs] = prod[1:2, :] + qk * v_new
            k_col = jnp.broadcast_to(k_row, (HEAD_DIM, HEAD_DIM)).T
            nds_ref[b, h] = s_old * eg_x[b:b + 1, hs] + k_col * v_new
    yb_ref[...] = _bdot(_gated_out_norm(o_ref[...], z_ref[...], ng_ref[...]), wout_ref[...])


def _delta_step_call(qkv, z, ba, sst, ds, scw, alog, dtb, ng, wout, layer, bb):
    bsz, qkv_dim = qkv.shape
    v_dim = z.shape[-1]
    d = wout.shape[-1]
    flat = sst.shape[-1]

    def wspec(w):
        return _const_spec((None,) + w.shape[1:], lambda i: (layer, 0, 0))

    state_block = (bb, N_HEADS, HEAD_DIM, HEAD_DIM)
    return pl.pallas_call(
        _delta_step_kernel,
        grid=(bsz // bb,),
        in_specs=[pl.BlockSpec((bb, qkv_dim), lambda i: (i, 0)),
                  pl.BlockSpec((bb, v_dim), lambda i: (i, 0)),
                  pl.BlockSpec((bb, LANES), lambda i: (i, 0)),
                  pl.BlockSpec((None, bb, flat), lambda i: (layer, i, 0)),
                  pl.BlockSpec((None,) + state_block, lambda i: (layer, i, 0, 0, 0)),
                  wspec(scw), wspec(alog), wspec(dtb), wspec(ng), wspec(wout)],
        out_specs=[pl.BlockSpec((bb, d), lambda i: (i, 0)),
                   pl.BlockSpec((bb, flat), lambda i: (i, 0)),
                   pl.BlockSpec(state_block, lambda i: (i, 0, 0, 0))],
        out_shape=[jax.ShapeDtypeStruct((bsz, d), F32),
                   jax.ShapeDtypeStruct((bsz, flat), F32),
                   jax.ShapeDtypeStruct((bsz, N_HEADS, HEAD_DIM, HEAD_DIM), F32)],
        scratch_shapes=[pltpu.VMEM((bb, v_dim), F32)],
        compiler_params=_params(1),
        name="delta_step",
    )(qkv, z, ba, sst, ds, scw, alog, dtb, ng, wout)


def _merge_ffn_kernel(x_ref, ya_ref, yb_ref, gate_ref, gt1_ref, sc2_ref, sh2_ref, gt2_ref, n2g_ref,
                      wm_ref, wfi_ref, wfo_ref, fin_ref, out_ref, *, ff_chunk, final_norm):
    d = x_ref.shape[-1]
    d_ff = wfo_ref.shape[0]
    half = gate_ref.shape[0] // 2
    gate_a = jnp.concatenate([gate_ref[c] for c in range(half)], axis=1)
    gate_b = jnp.concatenate([gate_ref[half + c] for c in range(half)], axis=1)
    merged = gate_a * ya_ref[...] + gate_b * yb_ref[...]
    x1 = x_ref[...] + gt1_ref[...] * _bdot(merged, wm_ref[...])
    h2 = x1 * lax.rsqrt(jnp.mean(x1 * x1, axis=-1, keepdims=True) + RMS_EPS) * n2g_ref[...]
    hb = (h2 * (1.0 + sc2_ref[...]) + sh2_ref[...]).astype(BF16)
    acc = jnp.zeros(x1.shape, F32)
    for c0 in range(0, d_ff, ff_chunk):
        gate = jnp.dot(hb, wfi_ref[:, c0:c0 + ff_chunk], preferred_element_type=F32)
        up = jnp.dot(hb, wfi_ref[:, d_ff + c0:d_ff + c0 + ff_chunk], preferred_element_type=F32)
        acc = acc + _bdot(_silu(gate) * up, wfo_ref[c0:c0 + ff_chunk, :])
    x2 = x1 + gt2_ref[...] * acc
    if final_norm:
        x2 = x2 * lax.rsqrt(jnp.mean(x2 * x2, axis=-1, keepdims=True) + RMS_EPS) * fin_ref[...]
    out_ref[...] = x2


def _merge_ffn_call(x, ya, yb, proj, n_plain, mod, n2g, wm, wfi, wfo, fin_g, layer, tm, final_norm):
    groups, rows, d = x.shape
    d_ff = wfo.shape[1]
    ff_chunk = 256
    n_gate, cw = proj.shape[1] - n_plain, proj.shape[3]
    assert d_ff % ff_chunk == 0 and n_gate * cw == 2 * d and n_plain % n_gate == 0

    def wspec(w):
        return _const_spec((None,) + w.shape[1:], lambda g, i: (layer, 0, 0))

    def rowspec(w):
        return pl.BlockSpec((None, tm, w), lambda g, i: (g, i, 0))

    return pl.pallas_call(
        functools.partial(_merge_ffn_kernel, ff_chunk=ff_chunk, final_norm=final_norm),
        grid=(groups, rows // tm),
        in_specs=[rowspec(d), rowspec(d), rowspec(d),
                  pl.BlockSpec((None, n_gate, tm, cw), lambda g, i: (g, n_plain // n_gate, i, 0)),
                  _mod_spec(mod, tm, 2), _mod_spec(mod, tm, 4), _mod_spec(mod, tm, 3), _mod_spec(mod, tm, 5),
                  wspec(n2g), wspec(wm), wspec(wfi), wspec(wfo),
                  _const_spec((1, d), lambda g, i: (0, 0))],
        out_specs=rowspec(d),
        out_shape=jax.ShapeDtypeStruct((groups, rows, d), F32),
        compiler_params=_params(2),
        name="merge_ffn",
    )(x, ya, yb, proj, mod, mod, mod, mod, n2g, wm, wfi, wfo, fin_g)


def kernel(x_prompt, x_sample, c_prompt, c_sample, state_conformer_conv, state_short_conv, state_delta, w_ada, b_ada, norm1_g, w_in, conf_dw_w, conf_dw_b, conf_ln_g, conf_ln_b, w_conf_out, short_conv_w, a_log, dt_bias, delta_norm_g, w_delta_out, w_merge_out, norm2_g, w_ffn_in, w_ffn_out, final_norm_g):
    depth, d, in_dim = w_in.shape
    bp, t_len, _ = x_prompt.shape
    bs = x_sample.shape[0]
    assert x_sample.shape[1] == 1
    conf_ch = conf_dw_w.shape[-1]
    qkv_dim = short_conv_w.shape[-1]
    v_dim = w_delta_out.shape[1]
    assert v_dim == N_HEADS * HEAD_DIM and qkv_dim == 3 * v_dim
    assert conf_dw_w.shape[1] == CONF_KW and short_conv_w.shape[1] == SHORT_KW

    o_glu = 2 * conf_ch
    o_qkv = o_glu + qkv_dim
    o_z = o_qkv + v_dim
    o_ba = o_z + 2 * N_HEADS
    assert in_dim == o_ba + 2 * d
    n_plain = (o_z - o_glu) // v_dim
    assert (o_z - o_glu) % v_dim == 0 and (2 * d) % v_dim == 0
    wglu, wba, wcat = _split_w_in_call(w_in, o_glu, o_z, o_ba, v_dim, rows_per_block=256)
    w_conf_out_b = _cast_call(w_conf_out, w_conf_out.shape[1])
    w_delta_out_b = _cast_call(w_delta_out, w_delta_out.shape[1])
    w_merge_b = _cast_call(w_merge_out, w_merge_out.shape[1])
    w_ffn_in_b = _cast_call(w_ffn_in, 256)
    w_ffn_out_b = _cast_call(w_ffn_out, w_ffn_out.shape[1] // 4)

    def row3(a):
        return a.reshape(depth, 1, a.shape[-1])

    norm1 = row3(norm1_g)
    norm2 = row3(norm2_g)
    dw_b = row3(conf_dw_b)
    ln_g = row3(conf_ln_g)
    ln_b = row3(conf_ln_b)
    alog = jnp.pad(a_log, ((0, 0), (N_HEADS, LANES - 2 * N_HEADS))).reshape(depth, 1, LANES)
    dtb = jnp.pad(dt_bias, ((0, 0), (N_HEADS, LANES - 2 * N_HEADS))).reshape(depth, 1, LANES)
    ng = row3(jnp.tile(delta_norm_g, (1, N_HEADS)))
    fin_g = final_norm_g.reshape(1, d)

    mod = _ada_call(jnp.concatenate([c_prompt, c_sample], axis=0), w_ada, b_ada)
    mod_p = mod[:, :bp].reshape(depth, bp, 1, 6 * d)
    mod_s = mod[:, bp:].reshape(depth, 1, bs, 6 * d)
    conf_state = state_conformer_conv.reshape(depth, bs, (CONF_KW - 1) * conf_ch)
    short_state = state_short_conv.reshape(depth, bs, (SHORT_KW - 1) * qkv_dim)

    xp = x_prompt
    xs = x_sample.reshape(1, bs, d)
    conf_p, conf_s, short_p, short_s, delta_p, delta_s = [], [], [], [], [], []
    for l in range(depth):
        last = l == depth - 1
        ya, proj, ba, cb = _inproj_conf_call(xp, mod_p[l], norm1, wglu, wcat, wba, conf_dw_w, dw_b, ln_g, ln_b,
                                             w_conf_out_b, l, tm=512, n_plain=n_plain)
        yb, sb, ds = _delta_seq_call(proj, ba, short_conv_w, alog, dtb, ng, w_delta_out_b, l, tm=512)
        xp = _merge_ffn_call(xp, ya, yb, proj, n_plain, mod_p[l], norm2, w_merge_b, w_ffn_in_b, w_ffn_out_b,
                             fin_g, l, tm=512, final_norm=last)
        conf_p.append(cb)
        short_p.append(sb)
        delta_p.append(ds)
        u, proj, ba = _inproj_call(xs, mod_s[l], norm1, wglu, wcat, wba, l, tm=bs, n_plain=n_plain)
        ya, cb = _conf_step_call(u[0], conf_state, conf_dw_w, dw_b, ln_g, ln_b, w_conf_out_b, l, bb=32)
        qkv = jnp.concatenate([proj[0, c] for c in range(n_plain - 1)], axis=-1)
        yb, sb, ds = _delta_step_call(qkv, proj[0, n_plain - 1], ba[0], short_state, state_delta, short_conv_w,
                                      alog, dtb, ng, w_delta_out_b, l, bb=8)
        xs = _merge_ffn_call(xs, ya[None], yb[None], proj, n_plain, mod_s[l], norm2, w_merge_b, w_ffn_in_b,
                             w_ffn_out_b, fin_g, l, tm=bs, final_norm=last)
        conf_s.append(cb.reshape(bs, CONF_KW - 1, conf_ch))
        short_s.append(sb.reshape(bs, SHORT_KW - 1, qkv_dim))
        delta_s.append(ds)
    return (xp, xs.reshape(bs, 1, d), jnp.stack(conf_p), jnp.stack(conf_s), jnp.stack(short_p),
            jnp.stack(short_s), jnp.stack(delta_p), jnp.stack(delta_s))
```

```python
import functools

import jax
import jax.numpy as jnp
from jax import lax
from jax.experimental import pallas as pl
from jax.experimental.pallas import tpu as pltpu

F32 = jnp.float32
BF16 = jnp.bfloat16

CONF_KW = 31
SHORT_KW = 4
N_HEADS = 4
HEAD_DIM = 128
RMS_EPS = 1e-6
LN_EPS = 1e-5
L2_EPS = 1e-6
DELTA_CHUNK = 64
LANES = 128
CONF_TAIL = 32
SHORT_TAIL = 8
VMEM_LIMIT = 56 * 1024 * 1024


def _bdot(a, b):
    return jnp.dot(a.astype(BF16), b.astype(BF16), preferred_element_type=F32)


def _bdot_nt(a, b):
    return lax.dot_general(a.astype(BF16), b.astype(BF16), (((1,), (1,)), ((), ())),
                           preferred_element_type=F32)


def _silu(x):
    return x * jax.nn.sigmoid(x)


def _softplus(x):
    return jnp.maximum(x, 0.0) + jnp.log1p(jnp.exp(-jnp.abs(x)))


def _const_spec(shape, index_map):
    return pl.BlockSpec(shape, index_map, pipeline_mode=pl.Buffered(1))


def _params(n_axes):
    return pltpu.CompilerParams(dimension_semantics=("arbitrary",) * n_axes,
                                vmem_limit_bytes=VMEM_LIMIT)


def _ada_kernel(c_ref, w_ref, b_ref, o_ref):
    c = c_ref[...]
    o_ref[...] = _bdot(_silu(c), w_ref[...]) + b_ref[...]


def _ada_call(c_all, w_ada, b_ada):
    depth, d, six_d = w_ada.shape
    rows = c_all.shape[0]
    tn = 1536
    return pl.pallas_call(
        _ada_kernel,
        grid=(depth, six_d // tn),
        in_specs=[
            pl.BlockSpec((rows, d), lambda l, j: (0, 0)),
            pl.BlockSpec((None, d, tn), lambda l, j: (l, 0, j)),
            pl.BlockSpec((None, 1, tn), lambda l, j: (l, 0, j)),
        ],
        out_specs=pl.BlockSpec((None, rows, tn), lambda l, j: (l, 0, j)),
        out_shape=jax.ShapeDtypeStruct((depth, rows, six_d), F32),
        compiler_params=_params(2),
        name="ada_mod",
    )(c_all, w_ada, b_ada.reshape(depth, 1, six_d))


def _inproj_kernel(x_ref, sc_ref, sh_ref, g_ref, wglu_ref, wqkv_ref, wz_ref, wba_ref, wm_ref,
                   u_ref, qkv_ref, z_ref, ba_ref, gate_ref, *, conf_ch):
    x = x_ref[...]
    h = x * lax.rsqrt(jnp.mean(x * x, axis=-1, keepdims=True) + RMS_EPS) * g_ref[...]
    h = h * (1.0 + sc_ref[...]) + sh_ref[...]
    hb = h.astype(BF16)
    glu = jnp.dot(hb, wglu_ref[...], preferred_element_type=F32)
    u_ref[...] = glu[:, :conf_ch] * jax.nn.sigmoid(glu[:, conf_ch:])
    qkv_ref[...] = jnp.dot(hb, wqkv_ref[...], preferred_element_type=F32)
    z_ref[...] = jnp.dot(hb, wz_ref[...], preferred_element_type=F32)
    ba_ref[...] = jnp.dot(hb, wba_ref[...], preferred_element_type=F32)
    gate_ref[...] = jax.nn.sigmoid(jnp.dot(hb, wm_ref[...], preferred_element_type=F32))


def _mod_spec(mod, tm, col):
    d = mod.shape[-1] // 6
    if mod.shape[1] == 1:
        return pl.BlockSpec((None, 1, d), lambda g, i: (g, 0, col))
    return pl.BlockSpec((None, tm, d), lambda g, i: (g, i, col))


def _inproj_call(x, mod, norm_g, wglu, wqkv, wz, wba, wm, layer, tm):
    groups, rows, d = x.shape
    conf_ch = wglu.shape[-1] // 2
    widths = (conf_ch, wqkv.shape[-1], wz.shape[-1], wba.shape[-1], wm.shape[-1])

    def wspec(w):
        return _const_spec((None,) + w.shape[1:], lambda g, i: (layer, 0, 0))

    return pl.pallas_call(
        functools.partial(_inproj_kernel, conf_ch=conf_ch),
        grid=(groups, rows // tm),
        in_specs=[
            pl.BlockSpec((None, tm, d), lambda g, i: (g, i, 0)),
            _mod_spec(mod, tm, 1),
            _mod_spec(mod, tm, 0),
            wspec(norm_g), wspec(wglu), wspec(wqkv), wspec(wz), wspec(wba), wspec(wm),
        ],
        out_specs=[pl.BlockSpec((None, tm, w), lambda g, i: (g, i, 0)) for w in widths],
        out_shape=[jax.ShapeDtypeStruct((groups, rows, w), F32) for w in widths],
        compiler_params=_params(2),
        name="in_proj",
    )(x, mod, mod, norm_g, wglu, wqkv, wz, wba, wm)


def _layer_norm_swish(ca, g, b):
    mu = jnp.mean(ca, axis=-1, keepdims=True)
    cen = ca - mu
    var = jnp.mean(cen * cen, axis=-1, keepdims=True)
    return _silu(cen * lax.rsqrt(var + LN_EPS) * g + b)


def _inproj_conf_kernel(x_ref, sc_ref, sh_ref, g_ref, wglu_ref, wqkv_ref, wz_ref, wba_ref, wm_ref,
                        dww_ref, dwb_ref, lng_ref, lnb_ref, wco_ref,
                        ya_ref, qkv_ref, z_ref, ba_ref, gate_ref, buf_ref,
                        ext_ref, ca_ref, *, conf_ch, tm, rc):
    t = pl.program_id(1)
    hist = CONF_KW - 1

    @pl.when(t == 0)
    def _():
        ext_ref[0:CONF_TAIL, :] = jnp.zeros((CONF_TAIL, conf_ch), F32)

    @pl.when(t > 0)
    def _():
        ext_ref[0:CONF_TAIL, :] = ext_ref[tm:tm + CONF_TAIL, :]

    x = x_ref[...]
    h = x * lax.rsqrt(jnp.mean(x * x, axis=-1, keepdims=True) + RMS_EPS) * g_ref[...]
    hb = (h * (1.0 + sc_ref[...]) + sh_ref[...]).astype(BF16)
    glu = jnp.dot(hb, wglu_ref[...], preferred_element_type=F32)
    ext_ref[CONF_TAIL:CONF_TAIL + tm, :] = glu[:, :conf_ch] * jax.nn.sigmoid(glu[:, conf_ch:])
    qkv_ref[...] = jnp.dot(hb, wqkv_ref[...], preferred_element_type=F32)
    z_ref[...] = jnp.dot(hb, wz_ref[...], preferred_element_type=F32)
    ba_ref[...] = jnp.dot(hb, wba_ref[...], preferred_element_type=F32)
    gate_ref[...] = jax.nn.sigmoid(jnp.dot(hb, wm_ref[...], preferred_element_type=F32))

    sub = 8
    base = CONF_TAIL - hist
    for r0 in range(0, tm, rc):
        y = None
        for r in range(sub):
            rows = rc + (sub if r else 0)
            acc = None
            for o in range(r if r >= base else r + sub, base + CONF_KW, sub):
                start = r0 + o - r
                term = ext_ref[start:start + rows, :] * dww_ref[o - base:o - base + 1, :]
                acc = term if acc is None else acc + term
            if r:
                acc = pltpu.roll(acc, rows - r, 0)[0:rc, :]
            y = acc if y is None else y + acc
        ca_ref[r0:r0 + rc, :] = y
    act = _layer_norm_swish(ca_ref[...] + dwb_ref[...], lng_ref[...], lnb_ref[...])
    ya_ref[...] = _bdot(act, wco_ref[...])

    @pl.when(t == pl.num_programs(1) - 1)
    def _():
        buf_ref[...] = ext_ref[tm + CONF_TAIL - hist:tm + CONF_TAIL, :]


def _inproj_conf_call(x, mod, norm_g, wglu, wqkv, wz, wba, wm, dw_w, dw_b, ln_g, ln_b, wco, layer, tm):
    bsz, t_len, d = x.shape
    conf_ch = wglu.shape[-1] // 2
    hist = CONF_KW - 1
    rc = 64
    assert t_len % tm == 0 and tm % rc == 0 and t_len >= hist
    widths = (d, wqkv.shape[-1], wz.shape[-1], wba.shape[-1], wm.shape[-1])

    def wspec(w):
        return _const_spec((None,) + w.shape[1:], lambda g, i: (layer, 0, 0))

    return pl.pallas_call(
        functools.partial(_inproj_conf_kernel, conf_ch=conf_ch, tm=tm, rc=rc),
        grid=(bsz, t_len // tm),
        in_specs=[
            pl.BlockSpec((None, tm, d), lambda g, i: (g, i, 0)),
            _mod_spec(mod, tm, 1),
            _mod_spec(mod, tm, 0),
            wspec(norm_g), wspec(wglu), wspec(wqkv), wspec(wz), wspec(wba), wspec(wm),
            wspec(dw_w), wspec(dw_b), wspec(ln_g), wspec(ln_b), wspec(wco),
        ],
        out_specs=[pl.BlockSpec((None, tm, w), lambda g, i: (g, i, 0)) for w in widths]
        + [pl.BlockSpec((None, hist, conf_ch), lambda g, i: (g, 0, 0))],
        out_shape=[jax.ShapeDtypeStruct((bsz, t_len, w), F32) for w in widths]
        + [jax.ShapeDtypeStruct((bsz, hist, conf_ch), F32)],
        scratch_shapes=[pltpu.VMEM((tm + CONF_TAIL, conf_ch), F32), pltpu.VMEM((tm, conf_ch), F32)],
        compiler_params=_params(2),
        name="in_proj_conformer",
    )(x, mod, mod, norm_g, wglu, wqkv, wz, wba, wm, dw_w, dw_b, ln_g, ln_b, wco)


def _conf_step_kernel(u_ref, st_ref, w_ref, b_ref, lng_ref, lnb_ref, wout_ref, ya_ref, nst_ref):
    ch = u_ref.shape[1]
    hist = CONF_KW - 1
    u = u_ref[...]
    acc = u * w_ref[hist:hist + 1, :]
    for j in range(hist):
        acc = acc + st_ref[:, j * ch:(j + 1) * ch] * w_ref[j:j + 1, :]
    act = _layer_norm_swish(acc + b_ref[...], lng_ref[...], lnb_ref[...])
    ya_ref[...] = _bdot(act, wout_ref[...])
    nst_ref[:, 0:(hist - 1) * ch] = st_ref[:, ch:hist * ch]
    nst_ref[:, (hist - 1) * ch:hist * ch] = u


def _conf_step_call(u, st, dw_w, dw_b, ln_g, ln_b, wout, layer, bb):
    bsz, ch = u.shape
    d = wout.shape[-1]
    flat = st.shape[-1]

    def wspec(w):
        return _const_spec((None,) + w.shape[1:], lambda i: (layer, 0, 0))

    return pl.pallas_call(
        _conf_step_kernel,
        grid=(bsz // bb,),
        in_specs=[pl.BlockSpec((bb, ch), lambda i: (i, 0)),
                  pl.BlockSpec((None, bb, flat), lambda i: (layer, i, 0)),
                  wspec(dw_w), wspec(dw_b), wspec(ln_g), wspec(ln_b), wspec(wout)],
        out_specs=[pl.BlockSpec((bb, d), lambda i: (i, 0)),
                   pl.BlockSpec((bb, flat), lambda i: (i, 0))],
        out_shape=[jax.ShapeDtypeStruct((bsz, d), F32),
                   jax.ShapeDtypeStruct((bsz, flat), F32)],
        compiler_params=_params(1),
        name="conformer_step",
    )(u, st, dw_w, dw_b, ln_g, ln_b, wout)


def _head_column(x, lane):
    ids = lax.broadcasted_iota(jnp.int32, x.shape, 1)
    return jnp.sum(jnp.where(ids == lane, x, 0.0), axis=-1, keepdims=True)


def _expand_heads(x, first_lane):
    rows = x.shape[0]
    return jnp.concatenate(
        [jnp.broadcast_to(_head_column(x, first_lane + h), (rows, HEAD_DIM)) for h in range(N_HEADS)], axis=1)


def _l2_normalize_heads(x):
    outs = []
    for h in range(N_HEADS):
        xh = x[:, h * HEAD_DIM:(h + 1) * HEAD_DIM]
        outs.append(xh * lax.rsqrt(jnp.sum(xh * xh, axis=-1, keepdims=True) + L2_EPS))
    return jnp.concatenate(outs, axis=1)


def _gated_out_norm(o, z, ng):
    outs = []
    for h in range(N_HEADS):
        oh = o[:, h * HEAD_DIM:(h + 1) * HEAD_DIM]
        outs.append(oh * lax.rsqrt(jnp.mean(oh * oh, axis=-1, keepdims=True) + RMS_EPS))
    return jnp.concatenate(outs, axis=1) * ng * _silu(z)


def _beta_and_log_decay(ba, alog, dtb):
    beta = jax.nn.sigmoid(ba)
    g = -jnp.exp(alog) * _softplus(ba + dtb)
    return beta, g


def _delta_seq_kernel(qkv_ref, z_ref, ba_ref, scw_ref, alog_ref, dtb_ref, ng_ref, wout_ref,
                      yb_ref, sbuf_ref, state_ref,
                      ext_ref, o_ref, q_ref, k_ref, kb_ref, rhs_ref, qg_ref, gx_ref,
                      lmat_ref, qk_ref, lt_ref, nt_ref, nmat_ref, *, tm):
    t = pl.program_id(1)
    n_seq = qkv_ref.shape[0]
    hist = SHORT_KW - 1
    qk_dim = N_HEADS * HEAD_DIM
    cl = DELTA_CHUNK
    seq_chunks = tm // cl
    n_chunks = n_seq * seq_chunks
    n_sys = n_chunks * N_HEADS

    @pl.when(t == 0)
    def _():
        for s in range(n_seq):
            ext_ref[s, 0:SHORT_TAIL, :] = jnp.zeros((SHORT_TAIL, ext_ref.shape[2]), F32)
        state_ref[...] = jnp.zeros(state_ref.shape, F32)

    @pl.when(t > 0)
    def _():
        for s in range(n_seq):
            ext_ref[s, 0:SHORT_TAIL, :] = ext_ref[s, tm:tm + SHORT_TAIL, :]

    convs = []
    for s in range(n_seq):
        ext_ref[s, SHORT_TAIL:SHORT_TAIL + tm, :] = qkv_ref[s]
        conv = ext_ref[s, SHORT_TAIL - hist:SHORT_TAIL - hist + tm, :] * scw_ref[0:1, :]
        for j in range(1, SHORT_KW):
            conv = conv + ext_ref[s, SHORT_TAIL - hist + j:SHORT_TAIL - hist + j + tm, :] * scw_ref[j:j + 1, :]
        convs.append(conv)
    act = _silu(jnp.concatenate(convs, axis=0))
    q = _l2_normalize_heads(act[:, 0:qk_dim]) * (HEAD_DIM ** -0.5)
    k = _l2_normalize_heads(act[:, qk_dim:2 * qk_dim])
    v = act[:, 2 * qk_dim:]

    ba = jnp.concatenate([ba_ref[s] for s in range(n_seq)], axis=0)
    beta, g = _beta_and_log_decay(ba, alog_ref[...], dtb_ref[...])
    pos = lax.broadcasted_iota(jnp.int32, g.shape, 0) % cl
    shift = 1
    while shift < cl:
        g = g + jnp.where(pos >= shift, pltpu.roll(g, shift, 0), 0.0)
        shift *= 2
    beta_x = _expand_heads(beta, 0)
    g_x = _expand_heads(g, N_HEADS)
    eg_x = jnp.exp(g_x)
    kb = k * beta_x
    q_ref[...] = q
    k_ref[...] = k
    kb_ref[...] = kb
    qg_ref[...] = q * eg_x
    gx_ref[...] = g_x
    rhs_u = v * beta_x
    rhs_w = kb * eg_x
    for h in range(N_HEADS):
        hs = slice(h * HEAD_DIM, (h + 1) * HEAD_DIM)
        rhs_ref[:, 2 * h * HEAD_DIM:(2 * h + 1) * HEAD_DIM] = rhs_u[:, hs]
        rhs_ref[:, (2 * h + 1) * HEAD_DIM:(2 * h + 2) * HEAD_DIM] = rhs_w[:, hs]

    row = lax.broadcasted_iota(jnp.int32, (cl, cl), 0)
    col = lax.broadcasted_iota(jnp.int32, (cl, cl), 1)
    causal = row >= col
    strict = row > col

    def build(c, carry):
        r0 = pl.multiple_of(c * cl, cl)
        for h in range(N_HEADS):
            hs = slice(h * HEAD_DIM, (h + 1) * HEAD_DIM)
            g_i = gx_ref[pl.ds(r0, cl), hs]
            g_j = g_i.T[0:cl, :]
            decay = jnp.where(causal, jnp.exp(jnp.where(causal, g_i[:, 0:cl] - g_j, 0.0)), 0.0)
            k_c = k_ref[pl.ds(r0, cl), hs]
            kk = _bdot_nt(kb_ref[pl.ds(r0, cl), hs], k_c)
            s0 = pl.multiple_of((c * N_HEADS + h) * cl, cl)
            lmat_ref[pl.ds(s0, cl), :] = jnp.where(strict, kk * decay, 0.0)
            qk_ref[pl.ds(s0, cl), :] = _bdot_nt(q_ref[pl.ds(r0, cl), hs], k_c) * decay
        return carry

    lax.fori_loop(0, n_chunks, build, 0, unroll=2)

    for i in range(1, cl):
        lt_ref[i] = lmat_ref[pl.ds(i, n_sys, stride=cl), :].T
    nt_ref[...] = jnp.zeros(nt_ref.shape, F32)
    sub = 8
    for i in range(1, cl):
        acc = [-lt_ref[i, sub * r:sub * (r + 1), :] for r in range((i - 1) // sub + 1)]
        for j in range(1, i):
            l_ij = lt_ref[i, j:j + 1, :]
            for r in range((j - 1) // sub + 1):
                acc[r] = acc[r] - l_ij * nt_ref[j, sub * r:sub * (r + 1), :]
        for r, a in enumerate(acc):
            nt_ref[i, sub * r:sub * (r + 1), :] = a
    for i in range(cl):
        nmat_ref[pl.ds(i, n_sys, stride=cl), :] = nt_ref[i].T

    def solve(c, carry):
        r0 = pl.multiple_of(c * cl, cl)
        for h in range(N_HEADS):
            s0 = pl.multiple_of((c * N_HEADS + h) * cl, cl)
            cols = slice(2 * h * HEAD_DIM, (2 * h + 2) * HEAD_DIM)
            rhs = rhs_ref[pl.ds(r0, cl), cols]
            n_s = nmat_ref[pl.ds(s0, cl), :]
            n_lo = n_s - n_s.astype(BF16).astype(F32)
            r_lo = rhs - rhs.astype(BF16).astype(F32)
            rhs_ref[pl.ds(r0, cl), cols] = rhs + _bdot(jnp.concatenate([n_s, n_lo, n_s], axis=1),
                                                       jnp.concatenate([rhs, rhs, r_lo], axis=0))
        return carry

    lax.fori_loop(0, n_chunks, solve, 0, unroll=2)

    def step(j, carry):
        for s in range(n_seq):
            c = s * seq_chunks + j
            r0 = pl.multiple_of(c * cl, cl)
            for h in range(N_HEADS):
                hs = slice(h * HEAD_DIM, (h + 1) * HEAD_DIM)
                s0 = pl.multiple_of((c * N_HEADS + h) * cl, cl)
                s_old = state_ref[s, h]
                ws_qs = _bdot(jnp.concatenate(
                    [rhs_ref[pl.ds(r0, cl), (2 * h + 1) * HEAD_DIM:(2 * h + 2) * HEAD_DIM],
                     qg_ref[pl.ds(r0, cl), hs]], axis=0), s_old)
                v_new = rhs_ref[pl.ds(r0, cl), 2 * h * HEAD_DIM:(2 * h + 1) * HEAD_DIM] - ws_qs[0:cl, :]
                g_i = gx_ref[pl.ds(r0, cl), hs]
                g_last = g_i[cl - 1:cl, :]
                k_dec = k_ref[pl.ds(r0, cl), hs] * jnp.exp(g_last - g_i)
                ov_sv = _bdot(jnp.concatenate([qk_ref[pl.ds(s0, cl), :], k_dec.T], axis=0), v_new)
                o_ref[pl.ds(r0, cl), hs] = ws_qs[cl:2 * cl, :] + ov_sv[0:cl, :]
                state_ref[s, h] = s_old * jnp.exp(g_last) + ov_sv[cl:, :]
        return carry

    lax.fori_loop(0, seq_chunks, step, 0)

    z = jnp.concatenate([z_ref[s] for s in range(n_seq)], axis=0)
    y = _bdot(_gated_out_norm(o_ref[...], z, ng_ref[...]), wout_ref[...])
    for s in range(n_seq):
        yb_ref[s] = y[s * tm:(s + 1) * tm, :]

    @pl.when(t == pl.num_programs(1) - 1)
    def _():
        for s in range(n_seq):
            sbuf_ref[s] = ext_ref[s, tm + SHORT_TAIL - hist:tm + SHORT_TAIL, :]


def _delta_seq_call(qkv, z, ba, scw, alog, dtb, ng, wout, layer, tm, n_seq):
    bsz, t_len, qkv_dim = qkv.shape
    v_dim = z.shape[-1]
    d = wout.shape[-1]
    hist = SHORT_KW - 1
    assert t_len % tm == 0 and tm % DELTA_CHUNK == 0 and bsz % n_seq == 0
    cl = DELTA_CHUNK
    rows = n_seq * tm
    n_sys = (rows // cl) * N_HEADS

    def wspec(w):
        return _const_spec((None,) + w.shape[1:], lambda b, t: (layer, 0, 0))

    return pl.pallas_call(
        functools.partial(_delta_seq_kernel, tm=tm),
        grid=(bsz // n_seq, t_len // tm),
        in_specs=[pl.BlockSpec((n_seq, tm, qkv_dim), lambda b, t: (b, t, 0)),
                  pl.BlockSpec((n_seq, tm, v_dim), lambda b, t: (b, t, 0)),
                  pl.BlockSpec((n_seq, tm, LANES), lambda b, t: (b, t, 0)),
                  wspec(scw), wspec(alog), wspec(dtb), wspec(ng), wspec(wout)],
        out_specs=[pl.BlockSpec((n_seq, tm, d), lambda b, t: (b, t, 0)),
                   pl.BlockSpec((n_seq, hist, qkv_dim), lambda b, t: (b, 0, 0)),
                   pl.BlockSpec((n_seq, N_HEADS, HEAD_DIM, HEAD_DIM), lambda b, t: (b, 0, 0, 0))],
        out_shape=[jax.ShapeDtypeStruct((bsz, t_len, d), F32),
                   jax.ShapeDtypeStruct((bsz, hist, qkv_dim), F32),
                   jax.ShapeDtypeStruct((bsz, N_HEADS, HEAD_DIM, HEAD_DIM), F32)],
        scratch_shapes=[pltpu.VMEM((n_seq, tm + SHORT_TAIL, qkv_dim), F32),
                        pltpu.VMEM((rows, v_dim), F32),
                        pltpu.VMEM((rows, v_dim), F32),
                        pltpu.VMEM((rows, v_dim), F32),
                        pltpu.VMEM((rows, v_dim), F32),
                        pltpu.VMEM((rows, 2 * v_dim), F32),
                        pltpu.VMEM((rows, v_dim), F32),
                        pltpu.VMEM((rows, v_dim), F32),
                        pltpu.VMEM((n_sys * cl, cl), F32),
                        pltpu.VMEM((n_sys * cl, cl), F32),
                        pltpu.VMEM((cl, cl, n_sys), F32),
                        pltpu.VMEM((cl, cl, n_sys), F32),
                        pltpu.VMEM((n_sys * cl, cl), F32)],
        compiler_params=_params(2),
        name="delta_seq",
    )(qkv, z, ba, scw, alog, dtb, ng, wout)


def _delta_step_kernel(qkv_ref, z_ref, ba_ref, sst_ref, ds_ref, scw_ref, alog_ref, dtb_ref, ng_ref, wout_ref,
                       yb_ref, nsst_ref, nds_ref, o_ref):
    bb, qkv_dim = qkv_ref.shape
    hist = SHORT_KW - 1
    qk_dim = N_HEADS * HEAD_DIM
    x = qkv_ref[...]
    conv = x * scw_ref[hist:hist + 1, :]
    for j in range(hist):
        conv = conv + sst_ref[:, j * qkv_dim:(j + 1) * qkv_dim] * scw_ref[j:j + 1, :]
    nsst_ref[:, 0:(hist - 1) * qkv_dim] = sst_ref[:, qkv_dim:hist * qkv_dim]
    nsst_ref[:, (hist - 1) * qkv_dim:hist * qkv_dim] = x
    act = _silu(conv)
    q = _l2_normalize_heads(act[:, 0:qk_dim]) * (HEAD_DIM ** -0.5)
    k = _l2_normalize_heads(act[:, qk_dim:2 * qk_dim])
    v = act[:, 2 * qk_dim:]
    beta, g = _beta_and_log_decay(ba_ref[...], alog_ref[...], dtb_ref[...])
    beta_x = _expand_heads(beta, 0)
    eg_x = jnp.exp(_expand_heads(g, N_HEADS))
    kb = k * beta_x
    u_all = v * beta_x
    w_all = kb * eg_x
    qg = q * eg_x
    sub = lax.broadcasted_iota(jnp.int32, (8, HEAD_DIM), 0)
    for b in range(bb):
        for h in range(N_HEADS):
            hs = slice(h * HEAD_DIM, (h + 1) * HEAD_DIM)
            s_old = ds_ref[b, h]
            lhs = jnp.where(sub == 0, w_all[b:b + 1, hs], jnp.where(sub == 1, qg[b:b + 1, hs], 0.0))
            prod = _bdot(lhs, s_old)
            v_new = u_all[b:b + 1, hs] - prod[0:1, :]
            k_row = k[b:b + 1, hs]
            qk = jnp.sum(q[b:b + 1, hs] * k_row, axis=-1, keepdims=True)
            o_ref[b:b + 1, hs] = prod[1:2, :] + qk * v_new
            k_col = jnp.broadcast_to(k_row, (HEAD_DIM, HEAD_DIM)).T
            nds_ref[b, h] = s_old * eg_x[b:b + 1, hs] + k_col * v_new
    yb_ref[...] = _bdot(_gated_out_norm(o_ref[...], z_ref[...], ng_ref[...]), wout_ref[...])


def _delta_step_call(qkv, z, ba, sst, ds, scw, alog, dtb, ng, wout, layer, bb):
    bsz, qkv_dim = qkv.shape
    v_dim = z.shape[-1]
    d = wout.shape[-1]
    flat = sst.shape[-1]

    def wspec(w):
        return _const_spec((None,) + w.shape[1:], lambda i: (layer, 0, 0))

    state_block = (bb, N_HEADS, HEAD_DIM, HEAD_DIM)
    return pl.pallas_call(
        _delta_step_kernel,
        grid=(bsz // bb,),
        in_specs=[pl.BlockSpec((bb, qkv_dim), lambda i: (i, 0)),
                  pl.BlockSpec((bb, v_dim), lambda i: (i, 0)),
                  pl.BlockSpec((bb, LANES), lambda i: (i, 0)),
                  pl.BlockSpec((None, bb, flat), lambda i: (layer, i, 0)),
                  pl.BlockSpec((None,) + state_block, lambda i: (layer, i, 0, 0, 0)),
                  wspec(scw), wspec(alog), wspec(dtb), wspec(ng), wspec(wout)],
        out_specs=[pl.BlockSpec((bb, d), lambda i: (i, 0)),
                   pl.BlockSpec((bb, flat), lambda i: (i, 0)),
                   pl.BlockSpec(state_block, lambda i: (i, 0, 0, 0))],
        out_shape=[jax.ShapeDtypeStruct((bsz, d), F32),
                   jax.ShapeDtypeStruct((bsz, flat), F32),
                   jax.ShapeDtypeStruct((bsz, N_HEADS, HEAD_DIM, HEAD_DIM), F32)],
        scratch_shapes=[pltpu.VMEM((bb, v_dim), F32)],
        compiler_params=_params(1),
        name="delta_step",
    )(qkv, z, ba, sst, ds, scw, alog, dtb, ng, wout)


def _merge_ffn_kernel(x_ref, ya_ref, yb_ref, gate_ref, gt1_ref, sc2_ref, sh2_ref, gt2_ref, n2g_ref,
                      wm_ref, wfi_ref, wfo_ref, fin_ref, out_ref, *, ff_chunk, final_norm):
    d = x_ref.shape[-1]
    d_ff = wfo_ref.shape[0]
    merged = gate_ref[:, 0:d] * ya_ref[...] + gate_ref[:, d:2 * d] * yb_ref[...]
    x1 = x_ref[...] + gt1_ref[...] * _bdot(merged, wm_ref[...])
    h2 = x1 * lax.rsqrt(jnp.mean(x1 * x1, axis=-1, keepdims=True) + RMS_EPS) * n2g_ref[...]
    hb = (h2 * (1.0 + sc2_ref[...]) + sh2_ref[...]).astype(BF16)
    acc = jnp.zeros(x1.shape, F32)
    for c0 in range(0, d_ff, ff_chunk):
        gate = jnp.dot(hb, wfi_ref[:, c0:c0 + ff_chunk], preferred_element_type=F32)
        up = jnp.dot(hb, wfi_ref[:, d_ff + c0:d_ff + c0 + ff_chunk], preferred_element_type=F32)
        acc = acc + _bdot(_silu(gate) * up, wfo_ref[c0:c0 + ff_chunk, :])
    x2 = x1 + gt2_ref[...] * acc
    if final_norm:
        x2 = x2 * lax.rsqrt(jnp.mean(x2 * x2, axis=-1, keepdims=True) + RMS_EPS) * fin_ref[...]
    out_ref[...] = x2


def _merge_ffn_call(x, ya, yb, gates, mod, n2g, wm, wfi, wfo, fin_g, layer, tm, final_norm):
    groups, rows, d = x.shape
    d_ff = wfo.shape[1]
    ff_chunk = 256
    assert d_ff % ff_chunk == 0

    def wspec(w):
        return _const_spec((None,) + w.shape[1:], lambda g, i: (layer, 0, 0))

    def rowspec(w):
        return pl.BlockSpec((None, tm, w), lambda g, i: (g, i, 0))

    return pl.pallas_call(
        functools.partial(_merge_ffn_kernel, ff_chunk=ff_chunk, final_norm=final_norm),
        grid=(groups, rows // tm),
        in_specs=[rowspec(d), rowspec(d), rowspec(d), rowspec(2 * d),
                  _mod_spec(mod, tm, 2), _mod_spec(mod, tm, 4), _mod_spec(mod, tm, 3), _mod_spec(mod, tm, 5),
                  wspec(n2g), wspec(wm), wspec(wfi), wspec(wfo),
                  _const_spec((1, d), lambda g, i: (0, 0))],
        out_specs=rowspec(d),
        out_shape=jax.ShapeDtypeStruct((groups, rows, d), F32),
        compiler_params=_params(2),
        name="merge_ffn",
    )(x, ya, yb, gates, mod, mod, mod, mod, n2g, wm, wfi, wfo, fin_g)


def kernel(x_prompt, x_sample, c_prompt, c_sample, state_conformer_conv, state_short_conv, state_delta, w_ada, b_ada, norm1_g, w_in, conf_dw_w, conf_dw_b, conf_ln_g, conf_ln_b, w_conf_out, short_conv_w, a_log, dt_bias, delta_norm_g, w_delta_out, w_merge_out, norm2_g, w_ffn_in, w_ffn_out, final_norm_g):
    depth, d, in_dim = w_in.shape
    bp, t_len, _ = x_prompt.shape
    bs = x_sample.shape[0]
    assert x_sample.shape[1] == 1
    conf_ch = conf_dw_w.shape[-1]
    qkv_dim = short_conv_w.shape[-1]
    v_dim = w_delta_out.shape[1]
    assert v_dim == N_HEADS * HEAD_DIM and qkv_dim == 3 * v_dim
    assert conf_dw_w.shape[1] == CONF_KW and short_conv_w.shape[1] == SHORT_KW

    o_glu = 2 * conf_ch
    o_qkv = o_glu + qkv_dim
    o_z = o_qkv + v_dim
    o_ba = o_z + 2 * N_HEADS
    assert in_dim == o_ba + 2 * d
    wglu = w_in[:, :, :o_glu].astype(BF16)
    wqkv = w_in[:, :, o_glu:o_qkv].astype(BF16)
    wz = w_in[:, :, o_qkv:o_z].astype(BF16)
    wba = jnp.pad(w_in[:, :, o_z:o_ba].astype(BF16), ((0, 0), (0, 0), (0, LANES - 2 * N_HEADS)))
    wmg = w_in[:, :, o_ba:].astype(BF16)
    w_conf_out_b = w_conf_out.astype(BF16)
    w_delta_out_b = w_delta_out.astype(BF16)
    w_merge_b = w_merge_out.astype(BF16)
    w_ffn_in_b = w_ffn_in.astype(BF16)
    w_ffn_out_b = w_ffn_out.astype(BF16)

    def row3(a):
        return a.reshape(depth, 1, a.shape[-1])

    norm1 = row3(norm1_g)
    norm2 = row3(norm2_g)
    dw_b = row3(conf_dw_b)
    ln_g = row3(conf_ln_g)
    ln_b = row3(conf_ln_b)
    alog = jnp.pad(a_log, ((0, 0), (N_HEADS, LANES - 2 * N_HEADS))).reshape(depth, 1, LANES)
    dtb = jnp.pad(dt_bias, ((0, 0), (N_HEADS, LANES - 2 * N_HEADS))).reshape(depth, 1, LANES)
    ng = row3(jnp.tile(delta_norm_g, (1, N_HEADS)))
    fin_g = final_norm_g.reshape(1, d)

    mod = _ada_call(jnp.concatenate([c_prompt, c_sample], axis=0), w_ada, b_ada)
    mod_p = mod[:, :bp].reshape(depth, bp, 1, 6 * d)
    mod_s = mod[:, bp:].reshape(depth, 1, bs, 6 * d)
    conf_state = state_conformer_conv.reshape(depth, bs, (CONF_KW - 1) * conf_ch)
    short_state = state_short_conv.reshape(depth, bs, (SHORT_KW - 1) * qkv_dim)

    xp = x_prompt
    xs = x_sample.reshape(1, bs, d)
    conf_p, conf_s, short_p, short_s, delta_p, delta_s = [], [], [], [], [], []
    for l in range(depth):
        last = l == depth - 1
        ya, qkv, z, ba, gates, cb = _inproj_conf_call(xp, mod_p[l], norm1, wglu, wqkv, wz, wba, wmg,
                                                      conf_dw_w, dw_b, ln_g, ln_b, w_conf_out_b, l, tm=512)
        yb, sb, ds = _delta_seq_call(qkv, z, ba, short_conv_w, alog, dtb, ng, w_delta_out_b, l, tm=256, n_seq=2)
        xp = _merge_ffn_call(xp, ya, yb, gates, mod_p[l], norm2, w_merge_b, w_ffn_in_b, w_ffn_out_b, fin_g,
                             l, tm=512, final_norm=last)
        conf_p.append(cb)
        short_p.append(sb)
        delta_p.append(ds)
        u, qkv, z, ba, gates = _inproj_call(xs, mod_s[l], norm1, wglu, wqkv, wz, wba, wmg, l, tm=bs)
        ya, cb = _conf_step_call(u[0], conf_state, conf_dw_w, dw_b, ln_g, ln_b, w_conf_out_b, l, bb=32)
        yb, sb, ds = _delta_step_call(qkv[0], z[0], ba[0], short_state, state_delta, short_conv_w, alog, dtb, ng,
                                      w_delta_out_b, l, bb=8)
        xs = _merge_ffn_call(xs, ya[None], yb[None], gates, mod_s[l], norm2, w_merge_b, w_ffn_in_b, w_ffn_out_b,
                             fin_g, l, tm=bs, final_norm=last)
        conf_s.append(cb.reshape(bs, CONF_KW - 1, conf_ch))
        short_s.append(sb.reshape(bs, SHORT_KW - 1, qkv_dim))
        delta_s.append(ds)
    return (xp, xs.reshape(bs, 1, d), jnp.stack(conf_p), jnp.stack(conf_s), jnp.stack(short_p),
            jnp.stack(short_s), jnp.stack(delta_p), jnp.stack(delta_s))
```

```python
import functools

import jax
import jax.numpy as jnp
from jax import lax
from jax.experimental import pallas as pl
from jax.experimental.pallas import tpu as pltpu

F32 = jnp.float32
BF16 = jnp.bfloat16

CONF_KW = 31
SHORT_KW = 4
N_HEADS = 4
HEAD_DIM = 128
RMS_EPS = 1e-6
LN_EPS = 1e-5
L2_EPS = 1e-6
DELTA_CHUNK = 64
LANES = 128
CONF_TAIL = 32
SHORT_TAIL = 8
VMEM_LIMIT = 56 * 1024 * 1024


def _bdot(a, b):
    return jnp.dot(a.astype(BF16), b.astype(BF16), preferred_element_type=F32)


def _bdot_nt(a, b):
    return lax.dot_general(a.astype(BF16), b.astype(BF16), (((1,), (1,)), ((), ())),
                           preferred_element_type=F32)


def _silu(x):
    return x * jax.nn.sigmoid(x)


def _softplus(x):
    return jnp.maximum(x, 0.0) + jnp.log1p(jnp.exp(-jnp.abs(x)))


def _const_spec(shape, index_map):
    return pl.BlockSpec(shape, index_map, pipeline_mode=pl.Buffered(1))


def _params(n_axes):
    return pltpu.CompilerParams(dimension_semantics=("arbitrary",) * n_axes,
                                vmem_limit_bytes=VMEM_LIMIT)


def _ada_kernel(c_ref, w_ref, b_ref, o_ref):
    c = c_ref[...]
    o_ref[...] = _bdot(_silu(c), w_ref[...]) + b_ref[...]


def _ada_call(c_all, w_ada, b_ada):
    depth, d, six_d = w_ada.shape
    rows = c_all.shape[0]
    tn = 1536
    return pl.pallas_call(
        _ada_kernel,
        grid=(depth, six_d // tn),
        in_specs=[
            pl.BlockSpec((rows, d), lambda l, j: (0, 0)),
            pl.BlockSpec((None, d, tn), lambda l, j: (l, 0, j)),
            pl.BlockSpec((None, 1, tn), lambda l, j: (l, 0, j)),
        ],
        out_specs=pl.BlockSpec((None, rows, tn), lambda l, j: (l, 0, j)),
        out_shape=jax.ShapeDtypeStruct((depth, rows, six_d), F32),
        compiler_params=_params(2),
        name="ada_mod",
    )(c_all, w_ada, b_ada.reshape(depth, 1, six_d))


def _inproj_kernel(x_ref, sc_ref, sh_ref, g_ref, wglu_ref, wqkv_ref, wz_ref, wba_ref, wm_ref,
                   u_ref, qkv_ref, z_ref, ba_ref, gate_ref, *, conf_ch):
    x = x_ref[...]
    h = x * lax.rsqrt(jnp.mean(x * x, axis=-1, keepdims=True) + RMS_EPS) * g_ref[...]
    h = h * (1.0 + sc_ref[...]) + sh_ref[...]
    hb = h.astype(BF16)
    glu = jnp.dot(hb, wglu_ref[...], preferred_element_type=F32)
    u_ref[...] = glu[:, :conf_ch] * jax.nn.sigmoid(glu[:, conf_ch:])
    qkv_ref[...] = jnp.dot(hb, wqkv_ref[...], preferred_element_type=F32)
    z_ref[...] = jnp.dot(hb, wz_ref[...], preferred_element_type=F32)
    ba_ref[...] = jnp.dot(hb, wba_ref[...], preferred_element_type=F32)
    gate_ref[...] = jax.nn.sigmoid(jnp.dot(hb, wm_ref[...], preferred_element_type=F32))


def _mod_spec(mod, tm, col):
    d = mod.shape[-1] // 6
    if mod.shape[1] == 1:
        return pl.BlockSpec((None, 1, d), lambda g, i: (g, 0, col))
    return pl.BlockSpec((None, tm, d), lambda g, i: (g, i, col))


def _inproj_call(x, mod, norm_g, wglu, wqkv, wz, wba, wm, layer, tm):
    groups, rows, d = x.shape
    conf_ch = wglu.shape[-1] // 2
    widths = (conf_ch, wqkv.shape[-1], wz.shape[-1], wba.shape[-1], wm.shape[-1])

    def wspec(w):
        return _const_spec((None,) + w.shape[1:], lambda g, i: (layer, 0, 0))

    return pl.pallas_call(
        functools.partial(_inproj_kernel, conf_ch=conf_ch),
        grid=(groups, rows // tm),
        in_specs=[
            pl.BlockSpec((None, tm, d), lambda g, i: (g, i, 0)),
            _mod_spec(mod, tm, 1),
            _mod_spec(mod, tm, 0),
            wspec(norm_g), wspec(wglu), wspec(wqkv), wspec(wz), wspec(wba), wspec(wm),
        ],
        out_specs=[pl.BlockSpec((None, tm, w), lambda g, i: (g, i, 0)) for w in widths],
        out_shape=[jax.ShapeDtypeStruct((groups, rows, w), F32) for w in widths],
        compiler_params=_params(2),
        name="in_proj",
    )(x, mod, mod, norm_g, wglu, wqkv, wz, wba, wm)


def _layer_norm_swish(ca, g, b):
    mu = jnp.mean(ca, axis=-1, keepdims=True)
    cen = ca - mu
    var = jnp.mean(cen * cen, axis=-1, keepdims=True)
    return _silu(cen * lax.rsqrt(var + LN_EPS) * g + b)


def _inproj_conf_kernel(x_ref, sc_ref, sh_ref, g_ref, wglu_ref, wqkv_ref, wz_ref, wba_ref, wm_ref,
                        dww_ref, dwb_ref, lng_ref, lnb_ref, wco_ref,
                        ya_ref, qkv_ref, z_ref, ba_ref, gate_ref, buf_ref,
                        ext_ref, ca_ref, *, conf_ch, tm, rc):
    t = pl.program_id(1)
    hist = CONF_KW - 1

    @pl.when(t == 0)
    def _():
        ext_ref[0:CONF_TAIL, :] = jnp.zeros((CONF_TAIL, conf_ch), F32)

    @pl.when(t > 0)
    def _():
        ext_ref[0:CONF_TAIL, :] = ext_ref[tm:tm + CONF_TAIL, :]

    x = x_ref[...]
    h = x * lax.rsqrt(jnp.mean(x * x, axis=-1, keepdims=True) + RMS_EPS) * g_ref[...]
    hb = (h * (1.0 + sc_ref[...]) + sh_ref[...]).astype(BF16)
    glu = jnp.dot(hb, wglu_ref[...], preferred_element_type=F32)
    ext_ref[CONF_TAIL:CONF_TAIL + tm, :] = glu[:, :conf_ch] * jax.nn.sigmoid(glu[:, conf_ch:])
    qkv_ref[...] = jnp.dot(hb, wqkv_ref[...], preferred_element_type=F32)
    z_ref[...] = jnp.dot(hb, wz_ref[...], preferred_element_type=F32)
    ba_ref[...] = jnp.dot(hb, wba_ref[...], preferred_element_type=F32)
    gate_ref[...] = jax.nn.sigmoid(jnp.dot(hb, wm_ref[...], preferred_element_type=F32))

    sub = 8
    base = CONF_TAIL - hist
    for r0 in range(0, tm, rc):
        y = None
        for r in range(sub):
            rows = rc + (sub if r else 0)
            acc = None
            for o in range(r if r >= base else r + sub, base + CONF_KW, sub):
                start = r0 + o - r
                term = ext_ref[start:start + rows, :] * dww_ref[o - base:o - base + 1, :]
                acc = term if acc is None else acc + term
            if r:
                acc = pltpu.roll(acc, rows - r, 0)[0:rc, :]
            y = acc if y is None else y + acc
        ca_ref[r0:r0 + rc, :] = y
    act = _layer_norm_swish(ca_ref[...] + dwb_ref[...], lng_ref[...], lnb_ref[...])
    ya_ref[...] = _bdot(act, wco_ref[...])

    @pl.when(t == pl.num_programs(1) - 1)
    def _():
        buf_ref[...] = ext_ref[tm + CONF_TAIL - hist:tm + CONF_TAIL, :]


def _inproj_conf_call(x, mod, norm_g, wglu, wqkv, wz, wba, wm, dw_w, dw_b, ln_g, ln_b, wco, layer, tm):
    bsz, t_len, d = x.shape
    conf_ch = wglu.shape[-1] // 2
    hist = CONF_KW - 1
    rc = 64
    assert t_len % tm == 0 and tm % rc == 0 and t_len >= hist
    widths = (d, wqkv.shape[-1], wz.shape[-1], wba.shape[-1], wm.shape[-1])

    def wspec(w):
        return _const_spec((None,) + w.shape[1:], lambda g, i: (layer, 0, 0))

    return pl.pallas_call(
        functools.partial(_inproj_conf_kernel, conf_ch=conf_ch, tm=tm, rc=rc),
        grid=(bsz, t_len // tm),
        in_specs=[
            pl.BlockSpec((None, tm, d), lambda g, i: (g, i, 0)),
            _mod_spec(mod, tm, 1),
            _mod_spec(mod, tm, 0),
            wspec(norm_g), wspec(wglu), wspec(wqkv), wspec(wz), wspec(wba), wspec(wm),
            wspec(dw_w), wspec(dw_b), wspec(ln_g), wspec(ln_b), wspec(wco),
        ],
        out_specs=[pl.BlockSpec((None, tm, w), lambda g, i: (g, i, 0)) for w in widths]
        + [pl.BlockSpec((None, hist, conf_ch), lambda g, i: (g, 0, 0))],
        out_shape=[jax.ShapeDtypeStruct((bsz, t_len, w), F32) for w in widths]
        + [jax.ShapeDtypeStruct((bsz, hist, conf_ch), F32)],
        scratch_shapes=[pltpu.VMEM((tm + CONF_TAIL, conf_ch), F32), pltpu.VMEM((tm, conf_ch), F32)],
        compiler_params=_params(2),
        name="in_proj_conformer",
    )(x, mod, mod, norm_g, wglu, wqkv, wz, wba, wm, dw_w, dw_b, ln_g, ln_b, wco)


def _conf_step_kernel(u_ref, st_ref, w_ref, b_ref, lng_ref, lnb_ref, wout_ref, ya_ref, nst_ref, ca_ref):
    bb = u_ref.shape[0]
    hist = CONF_KW - 1
    w_hist = w_ref[0:hist, :]
    w_new = w_ref[hist:hist + 1, :]
    for b in range(bb):
        u_row = u_ref[b:b + 1, :]
        ca_ref[b:b + 1, :] = jnp.sum(st_ref[b] * w_hist, axis=0, keepdims=True) + u_row * w_new
        nst_ref[b, 0:hist - 1, :] = st_ref[b, 1:hist, :]
        nst_ref[b, hist - 1:hist, :] = u_row
    act = _layer_norm_swish(ca_ref[...] + b_ref[...], lng_ref[...], lnb_ref[...])
    ya_ref[...] = _bdot(act, wout_ref[...])


def _conf_step_call(u, st, dw_w, dw_b, ln_g, ln_b, wout, layer, bb):
    bsz, ch = u.shape
    d = wout.shape[-1]
    hist = st.shape[2]

    def wspec(w):
        return _const_spec((None,) + w.shape[1:], lambda i: (layer, 0, 0))

    return pl.pallas_call(
        _conf_step_kernel,
        grid=(bsz // bb,),
        in_specs=[pl.BlockSpec((bb, ch), lambda i: (i, 0)),
                  pl.BlockSpec((None, bb, hist, ch), lambda i: (layer, i, 0, 0)),
                  wspec(dw_w), wspec(dw_b), wspec(ln_g), wspec(ln_b), wspec(wout)],
        out_specs=[pl.BlockSpec((bb, d), lambda i: (i, 0)),
                   pl.BlockSpec((bb, hist, ch), lambda i: (i, 0, 0))],
        out_shape=[jax.ShapeDtypeStruct((bsz, d), F32),
                   jax.ShapeDtypeStruct((bsz, hist, ch), F32)],
        scratch_shapes=[pltpu.VMEM((bb, ch), F32)],
        compiler_params=_params(1),
        name="conformer_step",
    )(u, st, dw_w, dw_b, ln_g, ln_b, wout)


def _head_column(x, lane):
    ids = lax.broadcasted_iota(jnp.int32, x.shape, 1)
    return jnp.sum(jnp.where(ids == lane, x, 0.0), axis=-1, keepdims=True)


def _expand_heads(x, first_lane):
    rows = x.shape[0]
    return jnp.concatenate(
        [jnp.broadcast_to(_head_column(x, first_lane + h), (rows, HEAD_DIM)) for h in range(N_HEADS)], axis=1)


def _l2_normalize_heads(x):
    outs = []
    for h in range(N_HEADS):
        xh = x[:, h * HEAD_DIM:(h + 1) * HEAD_DIM]
        outs.append(xh * lax.rsqrt(jnp.sum(xh * xh, axis=-1, keepdims=True) + L2_EPS))
    return jnp.concatenate(outs, axis=1)


def _gated_out_norm(o, z, ng):
    outs = []
    for h in range(N_HEADS):
        oh = o[:, h * HEAD_DIM:(h + 1) * HEAD_DIM]
        outs.append(oh * lax.rsqrt(jnp.mean(oh * oh, axis=-1, keepdims=True) + RMS_EPS))
    return jnp.concatenate(outs, axis=1) * ng * _silu(z)


def _beta_and_log_decay(ba, alog, dtb):
    beta = jax.nn.sigmoid(ba)
    g = -jnp.exp(alog) * _softplus(ba + dtb)
    return beta, g


def _delta_seq_kernel(qkv_ref, z_ref, ba_ref, scw_ref, alog_ref, dtb_ref, ng_ref, wout_ref,
                      yb_ref, sbuf_ref, state_ref,
                      ext_ref, o_ref, q_ref, k_ref, kb_ref, rhs_ref, qg_ref, gx_ref,
                      lmat_ref, qk_ref, lt_ref, nt_ref, nmat_ref, *, tm):
    t = pl.program_id(1)
    n_seq = qkv_ref.shape[0]
    hist = SHORT_KW - 1
    qk_dim = N_HEADS * HEAD_DIM
    cl = DELTA_CHUNK
    seq_chunks = tm // cl
    n_chunks = n_seq * seq_chunks
    n_sys = n_chunks * N_HEADS

    @pl.when(t == 0)
    def _():
        for s in range(n_seq):
            ext_ref[s, 0:SHORT_TAIL, :] = jnp.zeros((SHORT_TAIL, ext_ref.shape[2]), F32)
        state_ref[...] = jnp.zeros(state_ref.shape, F32)

    @pl.when(t > 0)
    def _():
        for s in range(n_seq):
            ext_ref[s, 0:SHORT_TAIL, :] = ext_ref[s, tm:tm + SHORT_TAIL, :]

    convs = []
    for s in range(n_seq):
        ext_ref[s, SHORT_TAIL:SHORT_TAIL + tm, :] = qkv_ref[s]
        conv = ext_ref[s, SHORT_TAIL - hist:SHORT_TAIL - hist + tm, :] * scw_ref[0:1, :]
        for j in range(1, SHORT_KW):
            conv = conv + ext_ref[s, SHORT_TAIL - hist + j:SHORT_TAIL - hist + j + tm, :] * scw_ref[j:j + 1, :]
        convs.append(conv)
    act = _silu(jnp.concatenate(convs, axis=0))
    q = _l2_normalize_heads(act[:, 0:qk_dim]) * (HEAD_DIM ** -0.5)
    k = _l2_normalize_heads(act[:, qk_dim:2 * qk_dim])
    v = act[:, 2 * qk_dim:]

    ba = jnp.concatenate([ba_ref[s] for s in range(n_seq)], axis=0)
    beta, g = _beta_and_log_decay(ba, alog_ref[...], dtb_ref[...])
    pos = lax.broadcasted_iota(jnp.int32, g.shape, 0) % cl
    shift = 1
    while shift < cl:
        g = g + jnp.where(pos >= shift, pltpu.roll(g, shift, 0), 0.0)
        shift *= 2
    beta_x = _expand_heads(beta, 0)
    g_x = _expand_heads(g, N_HEADS)
    eg_x = jnp.exp(g_x)
    kb = k * beta_x
    q_ref[...] = q
    k_ref[...] = k
    kb_ref[...] = kb
    qg_ref[...] = q * eg_x
    gx_ref[...] = g_x
    rhs_u = v * beta_x
    rhs_w = kb * eg_x
    for h in range(N_HEADS):
        hs = slice(h * HEAD_DIM, (h + 1) * HEAD_DIM)
        rhs_ref[:, 2 * h * HEAD_DIM:(2 * h + 1) * HEAD_DIM] = rhs_u[:, hs]
        rhs_ref[:, (2 * h + 1) * HEAD_DIM:(2 * h + 2) * HEAD_DIM] = rhs_w[:, hs]

    row = lax.broadcasted_iota(jnp.int32, (cl, cl), 0)
    col = lax.broadcasted_iota(jnp.int32, (cl, cl), 1)
    causal = row >= col
    strict = row > col

    def build(c, carry):
        r0 = pl.multiple_of(c * cl, cl)
        for h in range(N_HEADS):
            hs = slice(h * HEAD_DIM, (h + 1) * HEAD_DIM)
            g_i = gx_ref[pl.ds(r0, cl), hs]
            g_j = g_i.T[0:cl, :]
            decay = jnp.where(causal, jnp.exp(jnp.where(causal, g_i[:, 0:cl] - g_j, 0.0)), 0.0)
            k_c = k_ref[pl.ds(r0, cl), hs]
            kk = _bdot_nt(kb_ref[pl.ds(r0, cl), hs], k_c)
            s0 = pl.multiple_of((c * N_HEADS + h) * cl, cl)
            lmat_ref[pl.ds(s0, cl), :] = jnp.where(strict, kk * decay, 0.0)
            qk_ref[pl.ds(s0, cl), :] = _bdot_nt(q_ref[pl.ds(r0, cl), hs], k_c) * decay
        return carry

    lax.fori_loop(0, n_chunks, build, 0, unroll=2)

    for i in range(1, cl):
        lt_ref[i] = lmat_ref[pl.ds(i, n_sys, stride=cl), :].T
    nt_ref[...] = jnp.zeros(nt_ref.shape, F32)
    sub = 8
    for i in range(1, cl):
        acc = [-lt_ref[i, sub * r:sub * (r + 1), :] for r in range((i - 1) // sub + 1)]
        for j in range(1, i):
            l_ij = lt_ref[i, j:j + 1, :]
            for r in range((j - 1) // sub + 1):
                acc[r] = acc[r] - l_ij * nt_ref[j, sub * r:sub * (r + 1), :]
        for r, a in enumerate(acc):
            nt_ref[i, sub * r:sub * (r + 1), :] = a
    for i in range(cl):
        nmat_ref[pl.ds(i, n_sys, stride=cl), :] = nt_ref[i].T

    def solve(c, carry):
        r0 = pl.multiple_of(c * cl, cl)
        for h in range(N_HEADS):
            s0 = pl.multiple_of((c * N_HEADS + h) * cl, cl)
            cols = slice(2 * h * HEAD_DIM, (2 * h + 2) * HEAD_DIM)
            rhs = rhs_ref[pl.ds(r0, cl), cols]
            n_s = nmat_ref[pl.ds(s0, cl), :]
            n_lo = n_s - n_s.astype(BF16).astype(F32)
            r_lo = rhs - rhs.astype(BF16).astype(F32)
            rhs_ref[pl.ds(r0, cl), cols] = rhs + _bdot(jnp.concatenate([n_s, n_lo, n_s], axis=1),
                                                       jnp.concatenate([rhs, rhs, r_lo], axis=0))
        return carry

    lax.fori_loop(0, n_chunks, solve, 0, unroll=2)

    def step(j, carry):
        for s in range(n_seq):
            c = s * seq_chunks + j
            r0 = pl.multiple_of(c * cl, cl)
            for h in range(N_HEADS):
                hs = slice(h * HEAD_DIM, (h + 1) * HEAD_DIM)
                s0 = pl.multiple_of((c * N_HEADS + h) * cl, cl)
                s_old = state_ref[s, h]
                ws_qs = _bdot(jnp.concatenate(
                    [rhs_ref[pl.ds(r0, cl), (2 * h + 1) * HEAD_DIM:(2 * h + 2) * HEAD_DIM],
                     qg_ref[pl.ds(r0, cl), hs]], axis=0), s_old)
                v_new = rhs_ref[pl.ds(r0, cl), 2 * h * HEAD_DIM:(2 * h + 1) * HEAD_DIM] - ws_qs[0:cl, :]
                g_i = gx_ref[pl.ds(r0, cl), hs]
                g_last = g_i[cl - 1:cl, :]
                k_dec = k_ref[pl.ds(r0, cl), hs] * jnp.exp(g_last - g_i)
                ov_sv = _bdot(jnp.concatenate([qk_ref[pl.ds(s0, cl), :], k_dec.T], axis=0), v_new)
                o_ref[pl.ds(r0, cl), hs] = ws_qs[cl:2 * cl, :] + ov_sv[0:cl, :]
                state_ref[s, h] = s_old * jnp.exp(g_last) + ov_sv[cl:, :]
        return carry

    lax.fori_loop(0, seq_chunks, step, 0)

    z = jnp.concatenate([z_ref[s] for s in range(n_seq)], axis=0)
    y = _bdot(_gated_out_norm(o_ref[...], z, ng_ref[...]), wout_ref[...])
    for s in range(n_seq):
        yb_ref[s] = y[s * tm:(s + 1) * tm, :]

    @pl.when(t == pl.num_programs(1) - 1)
    def _():
        for s in range(n_seq):
            sbuf_ref[s] = ext_ref[s, tm + SHORT_TAIL - hist:tm + SHORT_TAIL, :]


def _delta_seq_call(qkv, z, ba, scw, alog, dtb, ng, wout, layer, tm, n_seq):
    bsz, t_len, qkv_dim = qkv.shape
    v_dim = z.shape[-1]
    d = wout.shape[-1]
    hist = SHORT_KW - 1
    assert t_len % tm == 0 and tm % DELTA_CHUNK == 0 and bsz % n_seq == 0
    cl = DELTA_CHUNK
    rows = n_seq * tm
    n_sys = (rows // cl) * N_HEADS

    def wspec(w):
        return _const_spec((None,) + w.shape[1:], lambda b, t: (layer, 0, 0))

    return pl.pallas_call(
        functools.partial(_delta_seq_kernel, tm=tm),
        grid=(bsz // n_seq, t_len // tm),
        in_specs=[pl.BlockSpec((n_seq, tm, qkv_dim), lambda b, t: (b, t, 0)),
                  pl.BlockSpec((n_seq, tm, v_dim), lambda b, t: (b, t, 0)),
                  pl.BlockSpec((n_seq, tm, LANES), lambda b, t: (b, t, 0)),
                  wspec(scw), wspec(alog), wspec(dtb), wspec(ng), wspec(wout)],
        out_specs=[pl.BlockSpec((n_seq, tm, d), lambda b, t: (b, t, 0)),
                   pl.BlockSpec((n_seq, hist, qkv_dim), lambda b, t: (b, 0, 0)),
                   pl.BlockSpec((n_seq, N_HEADS, HEAD_DIM, HEAD_DIM), lambda b, t: (b, 0, 0, 0))],
        out_shape=[jax.ShapeDtypeStruct((bsz, t_len, d), F32),
                   jax.ShapeDtypeStruct((bsz, hist, qkv_dim), F32),
                   jax.ShapeDtypeStruct((bsz, N_HEADS, HEAD_DIM, HEAD_DIM), F32)],
        scratch_shapes=[pltpu.VMEM((n_seq, tm + SHORT_TAIL, qkv_dim), F32),
                        pltpu.VMEM((rows, v_dim), F32),
                        pltpu.VMEM((rows, v_dim), F32),
                        pltpu.VMEM((rows, v_dim), F32),
                        pltpu.VMEM((rows, v_dim), F32),
                        pltpu.VMEM((rows, 2 * v_dim), F32),
                        pltpu.VMEM((rows, v_dim), F32),
                        pltpu.VMEM((rows, v_dim), F32),
                        pltpu.VMEM((n_sys * cl, cl), F32),
                        pltpu.VMEM((n_sys * cl, cl), F32),
                        pltpu.VMEM((cl, cl, n_sys), F32),
                        pltpu.VMEM((cl, cl, n_sys), F32),
                        pltpu.VMEM((n_sys * cl, cl), F32)],
        compiler_params=_params(2),
        name="delta_seq",
    )(qkv, z, ba, scw, alog, dtb, ng, wout)


def _delta_step_kernel(qkv_ref, z_ref, ba_ref, sst_ref, ds_ref, scw_ref, alog_ref, dtb_ref, ng_ref, wout_ref,
                       yb_ref, nsst_ref, nds_ref, o_ref, conv_ref):
    bb = qkv_ref.shape[0]
    hist = SHORT_KW - 1
    qk_dim = N_HEADS * HEAD_DIM
    w_hist = scw_ref[0:hist, :]
    w_new = scw_ref[hist:hist + 1, :]
    for b in range(bb):
        x_row = qkv_ref[b:b + 1, :]
        conv_ref[b:b + 1, :] = jnp.sum(sst_ref[b] * w_hist, axis=0, keepdims=True) + x_row * w_new
        nsst_ref[b, 0:hist - 1, :] = sst_ref[b, 1:hist, :]
        nsst_ref[b, hist - 1:hist, :] = x_row
    act = _silu(conv_ref[...])
    q = _l2_normalize_heads(act[:, 0:qk_dim]) * (HEAD_DIM ** -0.5)
    k = _l2_normalize_heads(act[:, qk_dim:2 * qk_dim])
    v = act[:, 2 * qk_dim:]
    beta, g = _beta_and_log_decay(ba_ref[...], alog_ref[...], dtb_ref[...])
    beta_x = _expand_heads(beta, 0)
    eg_x = jnp.exp(_expand_heads(g, N_HEADS))
    kb = k * beta_x
    u_all = v * beta_x
    w_all = kb * eg_x
    qg = q * eg_x
    sub = lax.broadcasted_iota(jnp.int32, (8, HEAD_DIM), 0)
    for b in range(bb):
        for h in range(N_HEADS):
            hs = slice(h * HEAD_DIM, (h + 1) * HEAD_DIM)
            s_old = ds_ref[b, h]
            lhs = jnp.where(sub == 0, w_all[b:b + 1, hs], jnp.where(sub == 1, qg[b:b + 1, hs], 0.0))
            prod = _bdot(lhs, s_old)
            v_new = u_all[b:b + 1, hs] - prod[0:1, :]
            k_row = k[b:b + 1, hs]
            qk = jnp.sum(q[b:b + 1, hs] * k_row, axis=-1, keepdims=True)
            o_ref[b:b + 1, hs] = prod[1:2, :] + qk * v_new
            k_col = jnp.broadcast_to(k_row, (HEAD_DIM, HEAD_DIM)).T
            nds_ref[b, h] = s_old * eg_x[b:b + 1, hs] + k_col * v_new
    yb_ref[...] = _bdot(_gated_out_norm(o_ref[...], z_ref[...], ng_ref[...]), wout_ref[...])


def _delta_step_call(qkv, z, ba, sst, ds, scw, alog, dtb, ng, wout, layer, bb):
    bsz, qkv_dim = qkv.shape
    v_dim = z.shape[-1]
    d = wout.shape[-1]
    hist = sst.shape[2]

    def wspec(w):
        return _const_spec((None,) + w.shape[1:], lambda i: (layer, 0, 0))

    state_block = (bb, N_HEADS, HEAD_DIM, HEAD_DIM)
    return pl.pallas_call(
        _delta_step_kernel,
        grid=(bsz // bb,),
        in_specs=[pl.BlockSpec((bb, qkv_dim), lambda i: (i, 0)),
                  pl.BlockSpec((bb, v_dim), lambda i: (i, 0)),
                  pl.BlockSpec((bb, LANES), lambda i: (i, 0)),
                  pl.BlockSpec((None, bb, hist, qkv_dim), lambda i: (layer, i, 0, 0)),
                  pl.BlockSpec((None,) + state_block, lambda i: (layer, i, 0, 0, 0)),
                  wspec(scw), wspec(alog), wspec(dtb), wspec(ng), wspec(wout)],
        out_specs=[pl.BlockSpec((bb, d), lambda i: (i, 0)),
                   pl.BlockSpec((bb, hist, qkv_dim), lambda i: (i, 0, 0)),
                   pl.BlockSpec(state_block, lambda i: (i, 0, 0, 0))],
        out_shape=[jax.ShapeDtypeStruct((bsz, d), F32),
                   jax.ShapeDtypeStruct((bsz, hist, qkv_dim), F32),
                   jax.ShapeDtypeStruct((bsz, N_HEADS, HEAD_DIM, HEAD_DIM), F32)],
        scratch_shapes=[pltpu.VMEM((bb, v_dim), F32), pltpu.VMEM((bb, qkv_dim), F32)],
        compiler_params=_params(1),
        name="delta_step",
    )(qkv, z, ba, sst, ds, scw, alog, dtb, ng, wout)


def _merge_ffn_kernel(x_ref, ya_ref, yb_ref, gate_ref, gt1_ref, sc2_ref, sh2_ref, gt2_ref, n2g_ref,
                      wm_ref, wfi_ref, wfo_ref, fin_ref, out_ref, *, ff_chunk, final_norm):
    d = x_ref.shape[-1]
    d_ff = wfo_ref.shape[0]
    merged = gate_ref[:, 0:d] * ya_ref[...] + gate_ref[:, d:2 * d] * yb_ref[...]
    x1 = x_ref[...] + gt1_ref[...] * _bdot(merged, wm_ref[...])
    h2 = x1 * lax.rsqrt(jnp.mean(x1 * x1, axis=-1, keepdims=True) + RMS_EPS) * n2g_ref[...]
    hb = (h2 * (1.0 + sc2_ref[...]) + sh2_ref[...]).astype(BF16)
    acc = jnp.zeros(x1.shape, F32)
    for c0 in range(0, d_ff, ff_chunk):
        gate = jnp.dot(hb, wfi_ref[:, c0:c0 + ff_chunk], preferred_element_type=F32)
        up = jnp.dot(hb, wfi_ref[:, d_ff + c0:d_ff + c0 + ff_chunk], preferred_element_type=F32)
        acc = acc + _bdot(_silu(gate) * up, wfo_ref[c0:c0 + ff_chunk, :])
    x2 = x1 + gt2_ref[...] * acc
    if final_norm:
        x2 = x2 * lax.rsqrt(jnp.mean(x2 * x2, axis=-1, keepdims=True) + RMS_EPS) * fin_ref[...]
    out_ref[...] = x2


def _merge_ffn_call(x, ya, yb, gates, mod, n2g, wm, wfi, wfo, fin_g, layer, tm, final_norm):
    groups, rows, d = x.shape
    d_ff = wfo.shape[1]
    ff_chunk = 256
    assert d_ff % ff_chunk == 0

    def wspec(w):
        return _const_spec((None,) + w.shape[1:], lambda g, i: (layer, 0, 0))

    def rowspec(w):
        return pl.BlockSpec((None, tm, w), lambda g, i: (g, i, 0))

    return pl.pallas_call(
        functools.partial(_merge_ffn_kernel, ff_chunk=ff_chunk, final_norm=final_norm),
        grid=(groups, rows // tm),
        in_specs=[rowspec(d), rowspec(d), rowspec(d), rowspec(2 * d),
                  _mod_spec(mod, tm, 2), _mod_spec(mod, tm, 4), _mod_spec(mod, tm, 3), _mod_spec(mod, tm, 5),
                  wspec(n2g), wspec(wm), wspec(wfi), wspec(wfo),
                  _const_spec((1, d), lambda g, i: (0, 0))],
        out_specs=rowspec(d),
        out_shape=jax.ShapeDtypeStruct((groups, rows, d), F32),
        compiler_params=_params(2),
        name="merge_ffn",
    )(x, ya, yb, gates, mod, mod, mod, mod, n2g, wm, wfi, wfo, fin_g)


def kernel(x_prompt, x_sample, c_prompt, c_sample, state_conformer_conv, state_short_conv, state_delta, w_ada, b_ada, norm1_g, w_in, conf_dw_w, conf_dw_b, conf_ln_g, conf_ln_b, w_conf_out, short_conv_w, a_log, dt_bias, delta_norm_g, w_delta_out, w_merge_out, norm2_g, w_ffn_in, w_ffn_out, final_norm_g):
    depth, d, in_dim = w_in.shape
    bp, t_len, _ = x_prompt.shape
    bs = x_sample.shape[0]
    assert x_sample.shape[1] == 1
    conf_ch = conf_dw_w.shape[-1]
    qkv_dim = short_conv_w.shape[-1]
    v_dim = w_delta_out.shape[1]
    assert v_dim == N_HEADS * HEAD_DIM and qkv_dim == 3 * v_dim
    assert conf_dw_w.shape[1] == CONF_KW and short_conv_w.shape[1] == SHORT_KW

    o_glu = 2 * conf_ch
    o_qkv = o_glu + qkv_dim
    o_z = o_qkv + v_dim
    o_ba = o_z + 2 * N_HEADS
    assert in_dim == o_ba + 2 * d
    wglu = w_in[:, :, :o_glu].astype(BF16)
    wqkv = w_in[:, :, o_glu:o_qkv].astype(BF16)
    wz = w_in[:, :, o_qkv:o_z].astype(BF16)
    wba = jnp.pad(w_in[:, :, o_z:o_ba].astype(BF16), ((0, 0), (0, 0), (0, LANES - 2 * N_HEADS)))
    wmg = w_in[:, :, o_ba:].astype(BF16)
    w_conf_out_b = w_conf_out.astype(BF16)
    w_delta_out_b = w_delta_out.astype(BF16)
    w_merge_b = w_merge_out.astype(BF16)
    w_ffn_in_b = w_ffn_in.astype(BF16)
    w_ffn_out_b = w_ffn_out.astype(BF16)

    def row3(a):
        return a.reshape(depth, 1, a.shape[-1])

    norm1 = row3(norm1_g)
    norm2 = row3(norm2_g)
    dw_b = row3(conf_dw_b)
    ln_g = row3(conf_ln_g)
    ln_b = row3(conf_ln_b)
    alog = jnp.pad(a_log, ((0, 0), (N_HEADS, LANES - 2 * N_HEADS))).reshape(depth, 1, LANES)
    dtb = jnp.pad(dt_bias, ((0, 0), (N_HEADS, LANES - 2 * N_HEADS))).reshape(depth, 1, LANES)
    ng = row3(jnp.tile(delta_norm_g, (1, N_HEADS)))
    fin_g = final_norm_g.reshape(1, d)

    mod = _ada_call(jnp.concatenate([c_prompt, c_sample], axis=0), w_ada, b_ada)
    mod_p = mod[:, :bp].reshape(depth, bp, 1, 6 * d)
    mod_s = mod[:, bp:].reshape(depth, 1, bs, 6 * d)
    xp = x_prompt
    xs = x_sample.reshape(1, bs, d)
    conf_p, conf_s, short_p, short_s, delta_p, delta_s = [], [], [], [], [], []
    for l in range(depth):
        last = l == depth - 1
        ya, qkv, z, ba, gates, cb = _inproj_conf_call(xp, mod_p[l], norm1, wglu, wqkv, wz, wba, wmg,
                                                      conf_dw_w, dw_b, ln_g, ln_b, w_conf_out_b, l, tm=512)
        yb, sb, ds = _delta_seq_call(qkv, z, ba, short_conv_w, alog, dtb, ng, w_delta_out_b, l, tm=256, n_seq=2)
        xp = _merge_ffn_call(xp, ya, yb, gates, mod_p[l], norm2, w_merge_b, w_ffn_in_b, w_ffn_out_b, fin_g,
                             l, tm=512, final_norm=last)
        conf_p.append(cb)
        short_p.append(sb)
        delta_p.append(ds)
        u, qkv, z, ba, gates = _inproj_call(xs, mod_s[l], norm1, wglu, wqkv, wz, wba, wmg, l, tm=bs)
        ya, cb = _conf_step_call(u[0], state_conformer_conv, conf_dw_w, dw_b, ln_g, ln_b, w_conf_out_b, l, bb=32)
        yb, sb, ds = _delta_step_call(qkv[0], z[0], ba[0], state_short_conv, state_delta, short_conv_w, alog, dtb,
                                      ng, w_delta_out_b, l, bb=8)
        xs = _merge_ffn_call(xs, ya[None], yb[None], gates, mod_s[l], norm2, w_merge_b, w_ffn_in_b, w_ffn_out_b,
                             fin_g, l, tm=bs, final_norm=last)
        conf_s.append(cb)
        short_s.append(sb)
        delta_s.append(ds)
    return (xp, xs.reshape(bs, 1, d), jnp.stack(conf_p), jnp.stack(conf_s), jnp.stack(short_p),
            jnp.stack(short_s), jnp.stack(delta_p), jnp.stack(delta_s))
```

```python
import functools

import jax
import jax.numpy as jnp
from jax import lax
from jax.experimental import pallas as pl
from jax.experimental.pallas import tpu as pltpu

F32 = jnp.float32
BF16 = jnp.bfloat16

CONF_KW = 31
SHORT_KW = 4
N_HEADS = 4
HEAD_DIM = 128
RMS_EPS = 1e-6
LN_EPS = 1e-5
L2_EPS = 1e-6
DELTA_CHUNK = 64
LANES = 128
CONF_TAIL = 32
SHORT_TAIL = 8
VMEM_LIMIT = 56 * 1024 * 1024


def _bdot(a, b):
    return jnp.dot(a.astype(BF16), b.astype(BF16), preferred_element_type=F32)


def _bdot_nt(a, b):
    return lax.dot_general(a.astype(BF16), b.astype(BF16), (((1,), (1,)), ((), ())),
                           preferred_element_type=F32)


def _silu(x):
    return x * jax.nn.sigmoid(x)


def _softplus(x):
    return jnp.maximum(x, 0.0) + jnp.log1p(jnp.exp(-jnp.abs(x)))


def _const_spec(shape, index_map):
    return pl.BlockSpec(shape, index_map, pipeline_mode=pl.Buffered(1))


def _params(n_axes):
    return pltpu.CompilerParams(dimension_semantics=("arbitrary",) * n_axes,
                                vmem_limit_bytes=VMEM_LIMIT)


def _split_w_in_kernel(w_ref, glu_ref, qkv_ref, z_ref, ba_ref, m_ref, *, o_glu, o_qkv, o_z, o_ba, piece):
    glu_ref[...] = w_ref[:, 0:o_glu].astype(BF16)
    qkv_ref[...] = w_ref[:, o_glu:o_qkv].astype(BF16)
    z_ref[...] = w_ref[:, o_qkv:o_z].astype(BF16)
    tail = w_ref[:, o_z:]
    off = o_ba - o_z
    lane = lax.broadcasted_iota(jnp.int32, (tail.shape[0], LANES), 1)
    ba_ref[...] = jnp.where(lane < off, tail[:, 0:LANES], 0.0).astype(BF16)
    for c0 in range(0, m_ref.shape[1], piece):
        m_ref[:, c0:c0 + piece] = tail[:, off + c0:off + c0 + piece].astype(BF16)


def _split_w_in_call(w_in, o_glu, o_qkv, o_z, o_ba, rows_per_block):
    depth, d, in_dim = w_in.shape
    widths = (o_glu, o_qkv - o_glu, o_z - o_qkv, LANES, in_dim - o_ba)
    piece = 512
    assert o_z % LANES == 0 and 0 < o_ba - o_z <= LANES and widths[-1] % piece == 0 and d % rows_per_block == 0
    return pl.pallas_call(
        functools.partial(_split_w_in_kernel, o_glu=o_glu, o_qkv=o_qkv, o_z=o_z, o_ba=o_ba, piece=piece),
        grid=(depth, d // rows_per_block),
        in_specs=[pl.BlockSpec((None, rows_per_block, in_dim), lambda l, i: (l, i, 0))],
        out_specs=[pl.BlockSpec((None, rows_per_block, w), lambda l, i: (l, i, 0)) for w in widths],
        out_shape=[jax.ShapeDtypeStruct((depth, d, w), BF16) for w in widths],
        compiler_params=_params(2),
        name="split_w_in",
    )(w_in)


def _ada_kernel(c_ref, w_ref, b_ref, o_ref):
    c = c_ref[...]
    o_ref[...] = _bdot(_silu(c), w_ref[...]) + b_ref[...]


def _ada_call(c_all, w_ada, b_ada):
    depth, d, six_d = w_ada.shape
    rows = c_all.shape[0]
    tn = 1536
    return pl.pallas_call(
        _ada_kernel,
        grid=(depth, six_d // tn),
        in_specs=[
            pl.BlockSpec((rows, d), lambda l, j: (0, 0)),
            pl.BlockSpec((None, d, tn), lambda l, j: (l, 0, j)),
            pl.BlockSpec((None, 1, tn), lambda l, j: (l, 0, j)),
        ],
        out_specs=pl.BlockSpec((None, rows, tn), lambda l, j: (l, 0, j)),
        out_shape=jax.ShapeDtypeStruct((depth, rows, six_d), F32),
        compiler_params=_params(2),
        name="ada_mod",
    )(c_all, w_ada, b_ada.reshape(depth, 1, six_d))


def _inproj_kernel(x_ref, sc_ref, sh_ref, g_ref, wglu_ref, wqkv_ref, wz_ref, wba_ref, wm_ref,
                   u_ref, qkv_ref, z_ref, ba_ref, gate_ref, *, conf_ch):
    x = x_ref[...]
    h = x * lax.rsqrt(jnp.mean(x * x, axis=-1, keepdims=True) + RMS_EPS) * g_ref[...]
    h = h * (1.0 + sc_ref[...]) + sh_ref[...]
    hb = h.astype(BF16)
    glu = jnp.dot(hb, wglu_ref[...], preferred_element_type=F32)
    u_ref[...] = glu[:, :conf_ch] * jax.nn.sigmoid(glu[:, conf_ch:])
    qkv_ref[...] = jnp.dot(hb, wqkv_ref[...], preferred_element_type=F32)
    z_ref[...] = jnp.dot(hb, wz_ref[...], preferred_element_type=F32)
    ba_ref[...] = jnp.dot(hb, wba_ref[...], preferred_element_type=F32)
    gate_ref[...] = jax.nn.sigmoid(jnp.dot(hb, wm_ref[...], preferred_element_type=F32))


def _mod_spec(mod, tm, col):
    d = mod.shape[-1] // 6
    if mod.shape[1] == 1:
        return pl.BlockSpec((None, 1, d), lambda g, i: (g, 0, col))
    return pl.BlockSpec((None, tm, d), lambda g, i: (g, i, col))


def _inproj_call(x, mod, norm_g, wglu, wqkv, wz, wba, wm, layer, tm):
    groups, rows, d = x.shape
    conf_ch = wglu.shape[-1] // 2
    widths = (conf_ch, wqkv.shape[-1], wz.shape[-1], wba.shape[-1], wm.shape[-1])

    def wspec(w):
        return _const_spec((None,) + w.shape[1:], lambda g, i: (layer, 0, 0))

    return pl.pallas_call(
        functools.partial(_inproj_kernel, conf_ch=conf_ch),
        grid=(groups, rows // tm),
        in_specs=[
            pl.BlockSpec((None, tm, d), lambda g, i: (g, i, 0)),
            _mod_spec(mod, tm, 1),
            _mod_spec(mod, tm, 0),
            wspec(norm_g), wspec(wglu), wspec(wqkv), wspec(wz), wspec(wba), wspec(wm),
        ],
        out_specs=[pl.BlockSpec((None, tm, w), lambda g, i: (g, i, 0)) for w in widths],
        out_shape=[jax.ShapeDtypeStruct((groups, rows, w), F32) for w in widths],
        compiler_params=_params(2),
        name="in_proj",
    )(x, mod, mod, norm_g, wglu, wqkv, wz, wba, wm)


def _layer_norm_swish(ca, g, b):
    mu = jnp.mean(ca, axis=-1, keepdims=True)
    cen = ca - mu
    var = jnp.mean(cen * cen, axis=-1, keepdims=True)
    return _silu(cen * lax.rsqrt(var + LN_EPS) * g + b)


def _inproj_conf_kernel(x_ref, sc_ref, sh_ref, g_ref, wglu_ref, wqkv_ref, wz_ref, wba_ref, wm_ref,
                        dww_ref, dwb_ref, lng_ref, lnb_ref, wco_ref,
                        ya_ref, qkv_ref, z_ref, ba_ref, gate_ref, buf_ref,
                        ext_ref, ca_ref, *, conf_ch, tm, rc):
    t = pl.program_id(1)
    hist = CONF_KW - 1

    @pl.when(t == 0)
    def _():
        ext_ref[0:CONF_TAIL, :] = jnp.zeros((CONF_TAIL, conf_ch), F32)

    @pl.when(t > 0)
    def _():
        ext_ref[0:CONF_TAIL, :] = ext_ref[tm:tm + CONF_TAIL, :]

    x = x_ref[...]
    h = x * lax.rsqrt(jnp.mean(x * x, axis=-1, keepdims=True) + RMS_EPS) * g_ref[...]
    hb = (h * (1.0 + sc_ref[...]) + sh_ref[...]).astype(BF16)
    glu = jnp.dot(hb, wglu_ref[...], preferred_element_type=F32)
    ext_ref[CONF_TAIL:CONF_TAIL + tm, :] = glu[:, :conf_ch] * jax.nn.sigmoid(glu[:, conf_ch:])
    qkv_ref[...] = jnp.dot(hb, wqkv_ref[...], preferred_element_type=F32)
    z_ref[...] = jnp.dot(hb, wz_ref[...], preferred_element_type=F32)
    ba_ref[...] = jnp.dot(hb, wba_ref[...], preferred_element_type=F32)
    gate_ref[...] = jax.nn.sigmoid(jnp.dot(hb, wm_ref[...], preferred_element_type=F32))

    sub = 8
    base = CONF_TAIL - hist
    for r0 in range(0, tm, rc):
        y = None
        for r in range(sub):
            rows = rc + (sub if r else 0)
            acc = None
            for o in range(r if r >= base else r + sub, base + CONF_KW, sub):
                start = r0 + o - r
                term = ext_ref[start:start + rows, :] * dww_ref[o - base:o - base + 1, :]
                acc = term if acc is None else acc + term
            if r:
                acc = pltpu.roll(acc, rows - r, 0)[0:rc, :]
            y = acc if y is None else y + acc
        ca_ref[r0:r0 + rc, :] = y
    act = _layer_norm_swish(ca_ref[...] + dwb_ref[...], lng_ref[...], lnb_ref[...])
    ya_ref[...] = _bdot(act, wco_ref[...])

    @pl.when(t == pl.num_programs(1) - 1)
    def _():
        buf_ref[...] = ext_ref[tm + CONF_TAIL - hist:tm + CONF_TAIL, :]


def _inproj_conf_call(x, mod, norm_g, wglu, wqkv, wz, wba, wm, dw_w, dw_b, ln_g, ln_b, wco, layer, tm):
    bsz, t_len, d = x.shape
    conf_ch = wglu.shape[-1] // 2
    hist = CONF_KW - 1
    rc = 64
    assert t_len % tm == 0 and tm % rc == 0 and t_len >= hist
    widths = (d, wqkv.shape[-1], wz.shape[-1], wba.shape[-1], wm.shape[-1])

    def wspec(w):
        return _const_spec((None,) + w.shape[1:], lambda g, i: (layer, 0, 0))

    return pl.pallas_call(
        functools.partial(_inproj_conf_kernel, conf_ch=conf_ch, tm=tm, rc=rc),
        grid=(bsz, t_len // tm),
        in_specs=[
            pl.BlockSpec((None, tm, d), lambda g, i: (g, i, 0)),
            _mod_spec(mod, tm, 1),
            _mod_spec(mod, tm, 0),
            wspec(norm_g), wspec(wglu), wspec(wqkv), wspec(wz), wspec(wba), wspec(wm),
            wspec(dw_w), wspec(dw_b), wspec(ln_g), wspec(ln_b), wspec(wco),
        ],
        out_specs=[pl.BlockSpec((None, tm, w), lambda g, i: (g, i, 0)) for w in widths]
        + [pl.BlockSpec((None, hist, conf_ch), lambda g, i: (g, 0, 0))],
        out_shape=[jax.ShapeDtypeStruct((bsz, t_len, w), F32) for w in widths]
        + [jax.ShapeDtypeStruct((bsz, hist, conf_ch), F32)],
        scratch_shapes=[pltpu.VMEM((tm + CONF_TAIL, conf_ch), F32), pltpu.VMEM((tm, conf_ch), F32)],
        compiler_params=_params(2),
        name="in_proj_conformer",
    )(x, mod, mod, norm_g, wglu, wqkv, wz, wba, wm, dw_w, dw_b, ln_g, ln_b, wco)


def _conf_step_kernel(u_ref, st_ref, w_ref, b_ref, lng_ref, lnb_ref, wout_ref, ya_ref, nst_ref, ca_ref):
    bb = u_ref.shape[0]
    hist = CONF_KW - 1
    w_hist = w_ref[0:hist, :]
    w_new = w_ref[hist:hist + 1, :]
    for b in range(bb):
        u_row = u_ref[b:b + 1, :]
        ca_ref[b:b + 1, :] = jnp.sum(st_ref[b] * w_hist, axis=0, keepdims=True) + u_row * w_new
        nst_ref[b, 0:hist - 1, :] = st_ref[b, 1:hist, :]
        nst_ref[b, hist - 1:hist, :] = u_row
    act = _layer_norm_swish(ca_ref[...] + b_ref[...], lng_ref[...], lnb_ref[...])
    ya_ref[...] = _bdot(act, wout_ref[...])


def _conf_step_call(u, st, dw_w, dw_b, ln_g, ln_b, wout, layer, bb):
    bsz, ch = u.shape
    d = wout.shape[-1]
    hist = st.shape[2]

    def wspec(w):
        return _const_spec((None,) + w.shape[1:], lambda i: (layer, 0, 0))

    return pl.pallas_call(
        _conf_step_kernel,
        grid=(bsz // bb,),
        in_specs=[pl.BlockSpec((bb, ch), lambda i: (i, 0)),
                  pl.BlockSpec((None, bb, hist, ch), lambda i: (layer, i, 0, 0)),
                  wspec(dw_w), wspec(dw_b), wspec(ln_g), wspec(ln_b), wspec(wout)],
        out_specs=[pl.BlockSpec((bb, d), lambda i: (i, 0)),
                   pl.BlockSpec((bb, hist, ch), lambda i: (i, 0, 0))],
        out_shape=[jax.ShapeDtypeStruct((bsz, d), F32),
                   jax.ShapeDtypeStruct((bsz, hist, ch), F32)],
        scratch_shapes=[pltpu.VMEM((bb, ch), F32)],
        compiler_params=_params(1),
        name="conformer_step",
    )(u, st, dw_w, dw_b, ln_g, ln_b, wout)


def _head_column(x, lane):
    ids = lax.broadcasted_iota(jnp.int32, x.shape, 1)
    return jnp.sum(jnp.where(ids == lane, x, 0.0), axis=-1, keepdims=True)


def _expand_heads(x, first_lane):
    rows = x.shape[0]
    return jnp.concatenate(
        [jnp.broadcast_to(_head_column(x, first_lane + h), (rows, HEAD_DIM)) for h in range(N_HEADS)], axis=1)


def _l2_normalize_heads(x):
    outs = []
    for h in range(N_HEADS):
        xh = x[:, h * HEAD_DIM:(h + 1) * HEAD_DIM]
        outs.append(xh * lax.rsqrt(jnp.sum(xh * xh, axis=-1, keepdims=True) + L2_EPS))
    return jnp.concatenate(outs, axis=1)


def _gated_out_norm(o, z, ng):
    outs = []
    for h in range(N_HEADS):
        oh = o[:, h * HEAD_DIM:(h + 1) * HEAD_DIM]
        outs.append(oh * lax.rsqrt(jnp.mean(oh * oh, axis=-1, keepdims=True) + RMS_EPS))
    return jnp.concatenate(outs, axis=1) * ng * _silu(z)


def _beta_and_log_decay(ba, alog, dtb):
    beta = jax.nn.sigmoid(ba)
    g = -jnp.exp(alog) * _softplus(ba + dtb)
    return beta, g


def _delta_seq_kernel(qkv_ref, z_ref, ba_ref, scw_ref, alog_ref, dtb_ref, ng_ref, wout_ref,
                      yb_ref, sbuf_ref, state_ref,
                      ext_ref, o_ref, q_ref, k_ref, kb_ref, rhs_ref, qg_ref, gx_ref,
                      lmat_ref, qk_ref, lt_ref, nt_ref, nmat_ref, *, tm):
    t = pl.program_id(1)
    n_seq = qkv_ref.shape[0]
    hist = SHORT_KW - 1
    qk_dim = N_HEADS * HEAD_DIM
    cl = DELTA_CHUNK
    seq_chunks = tm // cl
    n_chunks = n_seq * seq_chunks
    n_sys = n_chunks * N_HEADS

    @pl.when(t == 0)
    def _():
        for s in range(n_seq):
            ext_ref[s, 0:SHORT_TAIL, :] = jnp.zeros((SHORT_TAIL, ext_ref.shape[2]), F32)
        state_ref[...] = jnp.zeros(state_ref.shape, F32)

    @pl.when(t > 0)
    def _():
        for s in range(n_seq):
            ext_ref[s, 0:SHORT_TAIL, :] = ext_ref[s, tm:tm + SHORT_TAIL, :]

    convs = []
    for s in range(n_seq):
        ext_ref[s, SHORT_TAIL:SHORT_TAIL + tm, :] = qkv_ref[s]
        conv = ext_ref[s, SHORT_TAIL - hist:SHORT_TAIL - hist + tm, :] * scw_ref[0:1, :]
        for j in range(1, SHORT_KW):
            conv = conv + ext_ref[s, SHORT_TAIL - hist + j:SHORT_TAIL - hist + j + tm, :] * scw_ref[j:j + 1, :]
        convs.append(conv)
    act = _silu(jnp.concatenate(convs, axis=0))
    q = _l2_normalize_heads(act[:, 0:qk_dim]) * (HEAD_DIM ** -0.5)
    k = _l2_normalize_heads(act[:, qk_dim:2 * qk_dim])
    v = act[:, 2 * qk_dim:]

    ba = jnp.concatenate([ba_ref[s] for s in range(n_seq)], axis=0)
    beta, g = _beta_and_log_decay(ba, alog_ref[...], dtb_ref[...])
    pos = lax.broadcasted_iota(jnp.int32, g.shape, 0) % cl
    shift = 1
    while shift < cl:
        g = g + jnp.where(pos >= shift, pltpu.roll(g, shift, 0), 0.0)
        shift *= 2
    beta_x = _expand_heads(beta, 0)
    g_x = _expand_heads(g, N_HEADS)
    eg_x = jnp.exp(g_x)
    kb = k * beta_x
    q_ref[...] = q
    k_ref[...] = k
    kb_ref[...] = kb
    qg_ref[...] = q * eg_x
    gx_ref[...] = g_x
    rhs_u = v * beta_x
    rhs_w = kb * eg_x
    for h in range(N_HEADS):
        hs = slice(h * HEAD_DIM, (h + 1) * HEAD_DIM)
        rhs_ref[:, 2 * h * HEAD_DIM:(2 * h + 1) * HEAD_DIM] = rhs_u[:, hs]
        rhs_ref[:, (2 * h + 1) * HEAD_DIM:(2 * h + 2) * HEAD_DIM] = rhs_w[:, hs]

    row = lax.broadcasted_iota(jnp.int32, (cl, cl), 0)
    col = lax.broadcasted_iota(jnp.int32, (cl, cl), 1)
    causal = row >= col
    strict = row > col

    def build(c, carry):
        r0 = pl.multiple_of(c * cl, cl)
        for h in range(N_HEADS):
            hs = slice(h * HEAD_DIM, (h + 1) * HEAD_DIM)
            g_i = gx_ref[pl.ds(r0, cl), hs]
            g_j = g_i.T[0:cl, :]
            decay = jnp.where(causal, jnp.exp(jnp.where(causal, g_i[:, 0:cl] - g_j, 0.0)), 0.0)
            k_c = k_ref[pl.ds(r0, cl), hs]
            kk = _bdot_nt(kb_ref[pl.ds(r0, cl), hs], k_c)
            s0 = pl.multiple_of((c * N_HEADS + h) * cl, cl)
            lmat_ref[pl.ds(s0, cl), :] = jnp.where(strict, kk * decay, 0.0)
            qk_ref[pl.ds(s0, cl), :] = _bdot_nt(q_ref[pl.ds(r0, cl), hs], k_c) * decay
        return carry

    lax.fori_loop(0, n_chunks, build, 0, unroll=2)

    for i in range(1, cl):
        lt_ref[i] = lmat_ref[pl.ds(i, n_sys, stride=cl), :].T
    nt_ref[...] = jnp.zeros(nt_ref.shape, F32)
    sub = 8
    for i in range(1, cl):
        acc = [-lt_ref[i, sub * r:sub * (r + 1), :] for r in range((i - 1) // sub + 1)]
        for j in range(1, i):
            l_ij = lt_ref[i, j:j + 1, :]
            for r in range((j - 1) // sub + 1):
                acc[r] = acc[r] - l_ij * nt_ref[j, sub * r:sub * (r + 1), :]
        for r, a in enumerate(acc):
            nt_ref[i, sub * r:sub * (r + 1), :] = a
    for i in range(cl):
        nmat_ref[pl.ds(i, n_sys, stride=cl), :] = nt_ref[i].T

    def solve(c, carry):
        r0 = pl.multiple_of(c * cl, cl)
        for h in range(N_HEADS):
            s0 = pl.multiple_of((c * N_HEADS + h) * cl, cl)
            cols = slice(2 * h * HEAD_DIM, (2 * h + 2) * HEAD_DIM)
            rhs = rhs_ref[pl.ds(r0, cl), cols]
            n_s = nmat_ref[pl.ds(s0, cl), :]
            n_lo = n_s - n_s.astype(BF16).astype(F32)
            r_lo = rhs - rhs.astype(BF16).astype(F32)
            rhs_ref[pl.ds(r0, cl), cols] = rhs + _bdot(jnp.concatenate([n_s, n_lo, n_s], axis=1),
                                                       jnp.concatenate([rhs, rhs, r_lo], axis=0))
        return carry

    lax.fori_loop(0, n_chunks, solve, 0, unroll=2)

    def step(j, carry):
        for s in range(n_seq):
            c = s * seq_chunks + j
            r0 = pl.multiple_of(c * cl, cl)
            for h in range(N_HEADS):
                hs = slice(h * HEAD_DIM, (h + 1) * HEAD_DIM)
                s0 = pl.multiple_of((c * N_HEADS + h) * cl, cl)
                s_old = state_ref[s, h]
                ws_qs = _bdot(jnp.concatenate(
                    [rhs_ref[pl.ds(r0, cl), (2 * h + 1) * HEAD_DIM:(2 * h + 2) * HEAD_DIM],
                     qg_ref[pl.ds(r0, cl), hs]], axis=0), s_old)
                v_new = rhs_ref[pl.ds(r0, cl), 2 * h * HEAD_DIM:(2 * h + 1) * HEAD_DIM] - ws_qs[0:cl, :]
                g_i = gx_ref[pl.ds(r0, cl), hs]
                g_last = g_i[cl - 1:cl, :]
                k_dec = k_ref[pl.ds(r0, cl), hs] * jnp.exp(g_last - g_i)
                ov_sv = _bdot(jnp.concatenate([qk_ref[pl.ds(s0, cl), :], k_dec.T], axis=0), v_new)
                o_ref[pl.ds(r0, cl), hs] = ws_qs[cl:2 * cl, :] + ov_sv[0:cl, :]
                state_ref[s, h] = s_old * jnp.exp(g_last) + ov_sv[cl:, :]
        return carry

    lax.fori_loop(0, seq_chunks, step, 0)

    z = jnp.concatenate([z_ref[s] for s in range(n_seq)], axis=0)
    y = _bdot(_gated_out_norm(o_ref[...], z, ng_ref[...]), wout_ref[...])
    for s in range(n_seq):
        yb_ref[s] = y[s * tm:(s + 1) * tm, :]

    @pl.when(t == pl.num_programs(1) - 1)
    def _():
        for s in range(n_seq):
            sbuf_ref[s] = ext_ref[s, tm + SHORT_TAIL - hist:tm + SHORT_TAIL, :]


def _delta_seq_call(qkv, z, ba, scw, alog, dtb, ng, wout, layer, tm, n_seq):
    bsz, t_len, qkv_dim = qkv.shape
    v_dim = z.shape[-1]
    d = wout.shape[-1]
    hist = SHORT_KW - 1
    assert t_len % tm == 0 and tm % DELTA_CHUNK == 0 and bsz % n_seq == 0
    cl = DELTA_CHUNK
    rows = n_seq * tm
    n_sys = (rows // cl) * N_HEADS

    def wspec(w):
        return _const_spec((None,) + w.shape[1:], lambda b, t: (layer, 0, 0))

    return pl.pallas_call(
        functools.partial(_delta_seq_kernel, tm=tm),
        grid=(bsz // n_seq, t_len // tm),
        in_specs=[pl.BlockSpec((n_seq, tm, qkv_dim), lambda b, t: (b, t, 0)),
                  pl.BlockSpec((n_seq, tm, v_dim), lambda b, t: (b, t, 0)),
                  pl.BlockSpec((n_seq, tm, LANES), lambda b, t: (b, t, 0)),
                  wspec(scw), wspec(alog), wspec(dtb), wspec(ng), wspec(wout)],
        out_specs=[pl.BlockSpec((n_seq, tm, d), lambda b, t: (b, t, 0)),
                   pl.BlockSpec((n_seq, hist, qkv_dim), lambda b, t: (b, 0, 0)),
                   pl.BlockSpec((n_seq, N_HEADS, HEAD_DIM, HEAD_DIM), lambda b, t: (b, 0, 0, 0))],
        out_shape=[jax.ShapeDtypeStruct((bsz, t_len, d), F32),
                   jax.ShapeDtypeStruct((bsz, hist, qkv_dim), F32),
                   jax.ShapeDtypeStruct((bsz, N_HEADS, HEAD_DIM, HEAD_DIM), F32)],
        scratch_shapes=[pltpu.VMEM((n_seq, tm + SHORT_TAIL, qkv_dim), F32),
                        pltpu.VMEM((rows, v_dim), F32),
                        pltpu.VMEM((rows, v_dim), F32),
                        pltpu.VMEM((rows, v_dim), F32),
                        pltpu.VMEM((rows, v_dim), F32),
                        pltpu.VMEM((rows, 2 * v_dim), F32),
                        pltpu.VMEM((rows, v_dim), F32),
                        pltpu.VMEM((rows, v_dim), F32),
                        pltpu.VMEM((n_sys * cl, cl), F32),
                        pltpu.VMEM((n_sys * cl, cl), F32),
                        pltpu.VMEM((cl, cl, n_sys), F32),
                        pltpu.VMEM((cl, cl, n_sys), F32),
                        pltpu.VMEM((n_sys * cl, cl), F32)],
        compiler_params=_params(2),
        name="delta_seq",
    )(qkv, z, ba, scw, alog, dtb, ng, wout)


def _delta_step_kernel(qkv_ref, z_ref, ba_ref, sst_ref, ds_ref, scw_ref, alog_ref, dtb_ref, ng_ref, wout_ref,
                       yb_ref, nsst_ref, nds_ref, o_ref, conv_ref):
    bb = qkv_ref.shape[0]
    hist = SHORT_KW - 1
    qk_dim = N_HEADS * HEAD_DIM
    w_hist = scw_ref[0:hist, :]
    w_new = scw_ref[hist:hist + 1, :]
    for b in range(bb):
        x_row = qkv_ref[b:b + 1, :]
        conv_ref[b:b + 1, :] = jnp.sum(sst_ref[b] * w_hist, axis=0, keepdims=True) + x_row * w_new
        nsst_ref[b, 0:hist - 1, :] = sst_ref[b, 1:hist, :]
        nsst_ref[b, hist - 1:hist, :] = x_row
    act = _silu(conv_ref[...])
    q = _l2_normalize_heads(act[:, 0:qk_dim]) * (HEAD_DIM ** -0.5)
    k = _l2_normalize_heads(act[:, qk_dim:2 * qk_dim])
    v = act[:, 2 * qk_dim:]
    beta, g = _beta_and_log_decay(ba_ref[...], alog_ref[...], dtb_ref[...])
    beta_x = _expand_heads(beta, 0)
    eg_x = jnp.exp(_expand_heads(g, N_HEADS))
    kb = k * beta_x
    u_all = v * beta_x
    w_all = kb * eg_x
    qg = q * eg_x
    sub = lax.broadcasted_iota(jnp.int32, (8, HEAD_DIM), 0)
    for b in range(bb):
        for h in range(N_HEADS):
            hs = slice(h * HEAD_DIM, (h + 1) * HEAD_DIM)
            s_old = ds_ref[b, h]
            lhs = jnp.where(sub == 0, w_all[b:b + 1, hs], jnp.where(sub == 1, qg[b:b + 1, hs], 0.0))
            prod = _bdot(lhs, s_old)
            v_new = u_all[b:b + 1, hs] - prod[0:1, :]
            k_row = k[b:b + 1, hs]
            qk = jnp.sum(q[b:b + 1, hs] * k_row, axis=-1, keepdims=True)
            o_ref[b:b + 1, hs] = prod[1:2, :] + qk * v_new
            k_col = jnp.broadcast_to(k_row, (HEAD_DIM, HEAD_DIM)).T
            nds_ref[b, h] = s_old * eg_x[b:b + 1, hs] + k_col * v_new
    yb_ref[...] = _bdot(_gated_out_norm(o_ref[...], z_ref[...], ng_ref[...]), wout_ref[...])


def _delta_step_call(qkv, z, ba, sst, ds, scw, alog, dtb, ng, wout, layer, bb):
    bsz, qkv_dim = qkv.shape
    v_dim = z.shape[-1]
    d = wout.shape[-1]
    hist = sst.shape[2]

    def wspec(w):
        return _const_spec((None,) + w.shape[1:], lambda i: (layer, 0, 0))

    state_block = (bb, N_HEADS, HEAD_DIM, HEAD_DIM)
    return pl.pallas_call(
        _delta_step_kernel,
        grid=(bsz // bb,),
        in_specs=[pl.BlockSpec((bb, qkv_dim), lambda i: (i, 0)),
                  pl.BlockSpec((bb, v_dim), lambda i: (i, 0)),
                  pl.BlockSpec((bb, LANES), lambda i: (i, 0)),
                  pl.BlockSpec((None, bb, hist, qkv_dim), lambda i: (layer, i, 0, 0)),
                  pl.BlockSpec((None,) + state_block, lambda i: (layer, i, 0, 0, 0)),
                  wspec(scw), wspec(alog), wspec(dtb), wspec(ng), wspec(wout)],
        out_specs=[pl.BlockSpec((bb, d), lambda i: (i, 0)),
                   pl.BlockSpec((bb, hist, qkv_dim), lambda i: (i, 0, 0)),
                   pl.BlockSpec(state_block, lambda i: (i, 0, 0, 0))],
        out_shape=[jax.ShapeDtypeStruct((bsz, d), F32),
                   jax.ShapeDtypeStruct((bsz, hist, qkv_dim), F32),
                   jax.ShapeDtypeStruct((bsz, N_HEADS, HEAD_DIM, HEAD_DIM), F32)],
        scratch_shapes=[pltpu.VMEM((bb, v_dim), F32), pltpu.VMEM((bb, qkv_dim), F32)],
        compiler_params=_params(1),
        name="delta_step",
    )(qkv, z, ba, sst, ds, scw, alog, dtb, ng, wout)


def _merge_ffn_kernel(x_ref, ya_ref, yb_ref, gate_ref, gt1_ref, sc2_ref, sh2_ref, gt2_ref, n2g_ref,
                      wm_ref, wfi_ref, wfo_ref, fin_ref, out_ref, *, ff_chunk, final_norm):
    d = x_ref.shape[-1]
    d_ff = wfo_ref.shape[0]
    merged = gate_ref[:, 0:d] * ya_ref[...] + gate_ref[:, d:2 * d] * yb_ref[...]
    x1 = x_ref[...] + gt1_ref[...] * _bdot(merged, wm_ref[...])
    h2 = x1 * lax.rsqrt(jnp.mean(x1 * x1, axis=-1, keepdims=True) + RMS_EPS) * n2g_ref[...]
    hb = (h2 * (1.0 + sc2_ref[...]) + sh2_ref[...]).astype(BF16)
    acc = jnp.zeros(x1.shape, F32)
    for c0 in range(0, d_ff, ff_chunk):
        gate = jnp.dot(hb, wfi_ref[:, c0:c0 + ff_chunk], preferred_element_type=F32)
        up = jnp.dot(hb, wfi_ref[:, d_ff + c0:d_ff + c0 + ff_chunk], preferred_element_type=F32)
        acc = acc + _bdot(_silu(gate) * up, wfo_ref[c0:c0 + ff_chunk, :])
    x2 = x1 + gt2_ref[...] * acc
    if final_norm:
        x2 = x2 * lax.rsqrt(jnp.mean(x2 * x2, axis=-1, keepdims=True) + RMS_EPS) * fin_ref[...]
    out_ref[...] = x2


def _merge_ffn_call(x, ya, yb, gates, mod, n2g, wm, wfi, wfo, fin_g, layer, tm, final_norm):
    groups, rows, d = x.shape
    d_ff = wfo.shape[1]
    ff_chunk = 256
    assert d_ff % ff_chunk == 0

    def wspec(w):
        return _const_spec((None,) + w.shape[1:], lambda g, i: (layer, 0, 0))

    def rowspec(w):
        return pl.BlockSpec((None, tm, w), lambda g, i: (g, i, 0))

    return pl.pallas_call(
        functools.partial(_merge_ffn_kernel, ff_chunk=ff_chunk, final_norm=final_norm),
        grid=(groups, rows // tm),
        in_specs=[rowspec(d), rowspec(d), rowspec(d), rowspec(2 * d),
                  _mod_spec(mod, tm, 2), _mod_spec(mod, tm, 4), _mod_spec(mod, tm, 3), _mod_spec(mod, tm, 5),
                  wspec(n2g), wspec(wm), wspec(wfi), wspec(wfo),
                  _const_spec((1, d), lambda g, i: (0, 0))],
        out_specs=rowspec(d),
        out_shape=jax.ShapeDtypeStruct((groups, rows, d), F32),
        compiler_params=_params(2),
        name="merge_ffn",
    )(x, ya, yb, gates, mod, mod, mod, mod, n2g, wm, wfi, wfo, fin_g)


def kernel(x_prompt, x_sample, c_prompt, c_sample, state_conformer_conv, state_short_conv, state_delta, w_ada, b_ada, norm1_g, w_in, conf_dw_w, conf_dw_b, conf_ln_g, conf_ln_b, w_conf_out, short_conv_w, a_log, dt_bias, delta_norm_g, w_delta_out, w_merge_out, norm2_g, w_ffn_in, w_ffn_out, final_norm_g):
    depth, d, in_dim = w_in.shape
    bp, t_len, _ = x_prompt.shape
    bs = x_sample.shape[0]
    assert x_sample.shape[1] == 1
    conf_ch = conf_dw_w.shape[-1]
    qkv_dim = short_conv_w.shape[-1]
    v_dim = w_delta_out.shape[1]
    assert v_dim == N_HEADS * HEAD_DIM and qkv_dim == 3 * v_dim
    assert conf_dw_w.shape[1] == CONF_KW and short_conv_w.shape[1] == SHORT_KW

    o_glu = 2 * conf_ch
    o_qkv = o_glu + qkv_dim
    o_z = o_qkv + v_dim
    o_ba = o_z + 2 * N_HEADS
    assert in_dim == o_ba + 2 * d
    wglu, wqkv, wz, wba, wmg = _split_w_in_call(w_in, o_glu, o_qkv, o_z, o_ba, rows_per_block=256)
    w_conf_out_b = w_conf_out.astype(BF16)
    w_delta_out_b = w_delta_out.astype(BF16)
    w_merge_b = w_merge_out.astype(BF16)
    w_ffn_in_b = w_ffn_in.astype(BF16)
    w_ffn_out_b = w_ffn_out.astype(BF16)

    def row3(a):
        return a.reshape(depth, 1, a.shape[-1])

    norm1 = row3(norm1_g)
    norm2 = row3(norm2_g)
    dw_b = row3(conf_dw_b)
    ln_g = row3(conf_ln_g)
    ln_b = row3(conf_ln_b)
    alog = jnp.pad(a_log, ((0, 0), (N_HEADS, LANES - 2 * N_HEADS))).reshape(depth, 1, LANES)
    dtb = jnp.pad(dt_bias, ((0, 0), (N_HEADS, LANES - 2 * N_HEADS))).reshape(depth, 1, LANES)
    ng = row3(jnp.tile(delta_norm_g, (1, N_HEADS)))
    fin_g = final_norm_g.reshape(1, d)

    mod = _ada_call(jnp.concatenate([c_prompt, c_sample], axis=0), w_ada, b_ada)
    mod_p = mod[:, :bp].reshape(depth, bp, 1, 6 * d)
    mod_s = mod[:, bp:].reshape(depth, 1, bs, 6 * d)
    xp = x_prompt
    xs = x_sample.reshape(1, bs, d)
    conf_p, conf_s, short_p, short_s, delta_p, delta_s = [], [], [], [], [], []
    for l in range(depth):
        last = l == depth - 1
        ya, qkv, z, ba, gates, cb = _inproj_conf_call(xp, mod_p[l], norm1, wglu, wqkv, wz, wba, wmg,
                                                      conf_dw_w, dw_b, ln_g, ln_b, w_conf_out_b, l, tm=512)
        yb, sb, ds = _delta_seq_call(qkv, z, ba, short_conv_w, alog, dtb, ng, w_delta_out_b, l, tm=256, n_seq=2)
        xp = _merge_ffn_call(xp, ya, yb, gates, mod_p[l], norm2, w_merge_b, w_ffn_in_b, w_ffn_out_b, fin_g,
                             l, tm=512, final_norm=last)
        conf_p.append(cb)
        short_p.append(sb)
        delta_p.append(ds)
        u, qkv, z, ba, gates = _inproj_call(xs, mod_s[l], norm1, wglu, wqkv, wz, wba, wmg, l, tm=bs)
        ya, cb = _conf_step_call(u[0], state_conformer_conv, conf_dw_w, dw_b, ln_g, ln_b, w_conf_out_b, l, bb=32)
        yb, sb, ds = _delta_step_call(qkv[0], z[0], ba[0], state_short_conv, state_delta, short_conv_w, alog, dtb,
                                      ng, w_delta_out_b, l, bb=8)
        xs = _merge_ffn_call(xs, ya[None], yb[None], gates, mod_s[l], norm2, w_merge_b, w_ffn_in_b, w_ffn_out_b,
                             fin_g, l, tm=bs, final_norm=last)
        conf_s.append(cb)
        short_s.append(sb)
        delta_s.append(ds)
    return (xp, xs.reshape(bs, 1, d), jnp.stack(conf_p), jnp.stack(conf_s), jnp.stack(short_p),
            jnp.stack(short_s), jnp.stack(delta_p), jnp.stack(delta_s))
```

```python
import functools

import jax
import jax.numpy as jnp
from jax import lax
from jax.experimental import pallas as pl
from jax.experimental.pallas import tpu as pltpu

F32 = jnp.float32
BF16 = jnp.bfloat16

CONF_KW = 31
SHORT_KW = 4
N_HEADS = 4
HEAD_DIM = 128
RMS_EPS = 1e-6
LN_EPS = 1e-5
L2_EPS = 1e-6
DELTA_CHUNK = 64
LANES = 128
CONF_TAIL = 32
SHORT_TAIL = 8
VMEM_LIMIT = 56 * 1024 * 1024


def _bdot(a, b):
    return jnp.dot(a.astype(BF16), b.astype(BF16), preferred_element_type=F32)


def _bdot_nt(a, b):
    return lax.dot_general(a.astype(BF16), b.astype(BF16), (((1,), (1,)), ((), ())),
                           preferred_element_type=F32)


def _silu(x):
    return x * jax.nn.sigmoid(x)


def _softplus(x):
    return jnp.maximum(x, 0.0) + jnp.log1p(jnp.exp(-jnp.abs(x)))


def _const_spec(shape, index_map):
    return pl.BlockSpec(shape, index_map, pipeline_mode=pl.Buffered(1))


def _params(n_axes):
    return pltpu.CompilerParams(dimension_semantics=("arbitrary",) * n_axes,
                                vmem_limit_bytes=VMEM_LIMIT)


def _ada_kernel(c_ref, w_ref, b_ref, o_ref):
    c = c_ref[...]
    o_ref[...] = _bdot(_silu(c), w_ref[...]) + b_ref[...]


def _ada_call(c_all, w_ada, b_ada):
    depth, d, six_d = w_ada.shape
    rows = c_all.shape[0]
    tn = 1536
    return pl.pallas_call(
        _ada_kernel,
        grid=(depth, six_d // tn),
        in_specs=[
            pl.BlockSpec((rows, d), lambda l, j: (0, 0)),
            pl.BlockSpec((None, d, tn), lambda l, j: (l, 0, j)),
            pl.BlockSpec((None, 1, tn), lambda l, j: (l, 0, j)),
        ],
        out_specs=pl.BlockSpec((None, rows, tn), lambda l, j: (l, 0, j)),
        out_shape=jax.ShapeDtypeStruct((depth, rows, six_d), F32),
        compiler_params=_params(2),
        name="ada_mod",
    )(c_all, w_ada, b_ada.reshape(depth, 1, six_d))


def _inproj_kernel(x_ref, sc_ref, sh_ref, g_ref, wglu_ref, wqkv_ref, wz_ref, wba_ref, wm_ref,
                   u_ref, qkv_ref, z_ref, ba_ref, gate_ref, *, conf_ch):
    x = x_ref[...]
    h = x * lax.rsqrt(jnp.mean(x * x, axis=-1, keepdims=True) + RMS_EPS) * g_ref[...]
    h = h * (1.0 + sc_ref[...]) + sh_ref[...]
    hb = h.astype(BF16)
    glu = jnp.dot(hb, wglu_ref[...], preferred_element_type=F32)
    u_ref[...] = glu[:, :conf_ch] * jax.nn.sigmoid(glu[:, conf_ch:])
    qkv_ref[...] = jnp.dot(hb, wqkv_ref[...], preferred_element_type=F32)
    z_ref[...] = jnp.dot(hb, wz_ref[...], preferred_element_type=F32)
    ba_ref[...] = jnp.dot(hb, wba_ref[...], preferred_element_type=F32)
    gate_ref[...] = jax.nn.sigmoid(jnp.dot(hb, wm_ref[...], preferred_element_type=F32))


def _mod_spec(mod, tm, col):
    d = mod.shape[-1] // 6
    if mod.shape[1] == 1:
        return pl.BlockSpec((None, 1, d), lambda g, i: (g, 0, col))
    return pl.BlockSpec((None, tm, d), lambda g, i: (g, i, col))


def _inproj_call(x, mod, norm_g, wglu, wqkv, wz, wba, wm, layer, tm):
    groups, rows, d = x.shape
    conf_ch = wglu.shape[-1] // 2
    widths = (conf_ch, wqkv.shape[-1], wz.shape[-1], wba.shape[-1], wm.shape[-1])

    def wspec(w):
        return _const_spec((None,) + w.shape[1:], lambda g, i: (layer, 0, 0))

    return pl.pallas_call(
        functools.partial(_inproj_kernel, conf_ch=conf_ch),
        grid=(groups, rows // tm),
        in_specs=[
            pl.BlockSpec((None, tm, d), lambda g, i: (g, i, 0)),
            _mod_spec(mod, tm, 1),
            _mod_spec(mod, tm, 0),
            wspec(norm_g), wspec(wglu), wspec(wqkv), wspec(wz), wspec(wba), wspec(wm),
        ],
        out_specs=[pl.BlockSpec((None, tm, w), lambda g, i: (g, i, 0)) for w in widths],
        out_shape=[jax.ShapeDtypeStruct((groups, rows, w), F32) for w in widths],
        compiler_params=_params(2),
        name="in_proj",
    )(x, mod, mod, norm_g, wglu, wqkv, wz, wba, wm)


def _layer_norm_swish(ca, g, b):
    mu = jnp.mean(ca, axis=-1, keepdims=True)
    cen = ca - mu
    var = jnp.mean(cen * cen, axis=-1, keepdims=True)
    return _silu(cen * lax.rsqrt(var + LN_EPS) * g + b)


def _inproj_conf_kernel(x_ref, sc_ref, sh_ref, g_ref, wglu_ref, wqkv_ref, wz_ref, wba_ref, wm_ref,
                        dww_ref, dwb_ref, lng_ref, lnb_ref, wco_ref,
                        ya_ref, qkv_ref, z_ref, ba_ref, gate_ref, buf_ref,
                        ext_ref, ca_ref, *, conf_ch, tm, rc):
    t = pl.program_id(1)
    hist = CONF_KW - 1

    @pl.when(t == 0)
    def _():
        ext_ref[0:CONF_TAIL, :] = jnp.zeros((CONF_TAIL, conf_ch), F32)

    @pl.when(t > 0)
    def _():
        ext_ref[0:CONF_TAIL, :] = ext_ref[tm:tm + CONF_TAIL, :]

    x = x_ref[...]
    h = x * lax.rsqrt(jnp.mean(x * x, axis=-1, keepdims=True) + RMS_EPS) * g_ref[...]
    hb = (h * (1.0 + sc_ref[...]) + sh_ref[...]).astype(BF16)
    glu = jnp.dot(hb, wglu_ref[...], preferred_element_type=F32)
    ext_ref[CONF_TAIL:CONF_TAIL + tm, :] = glu[:, :conf_ch] * jax.nn.sigmoid(glu[:, conf_ch:])
    qkv_ref[...] = jnp.dot(hb, wqkv_ref[...], preferred_element_type=F32)
    z_ref[...] = jnp.dot(hb, wz_ref[...], preferred_element_type=F32)
    ba_ref[...] = jnp.dot(hb, wba_ref[...], preferred_element_type=F32)
    gate_ref[...] = jax.nn.sigmoid(jnp.dot(hb, wm_ref[...], preferred_element_type=F32))

    sub = 8
    base = CONF_TAIL - hist
    for r0 in range(0, tm, rc):
        y = None
        for r in range(sub):
            rows = rc + (sub if r else 0)
            acc = None
            for o in range(r if r >= base else r + sub, base + CONF_KW, sub):
                start = r0 + o - r
                term = ext_ref[start:start + rows, :] * dww_ref[o - base:o - base + 1, :]
                acc = term if acc is None else acc + term
            if r:
                acc = pltpu.roll(acc, rows - r, 0)[0:rc, :]
            y = acc if y is None else y + acc
        ca_ref[r0:r0 + rc, :] = y
    act = _layer_norm_swish(ca_ref[...] + dwb_ref[...], lng_ref[...], lnb_ref[...])
    ya_ref[...] = _bdot(act, wco_ref[...])

    @pl.when(t == pl.num_programs(1) - 1)
    def _():
        buf_ref[...] = ext_ref[tm + CONF_TAIL - hist:tm + CONF_TAIL, :]


def _inproj_conf_call(x, mod, norm_g, wglu, wqkv, wz, wba, wm, dw_w, dw_b, ln_g, ln_b, wco, layer, tm):
    bsz, t_len, d = x.shape
    conf_ch = wglu.shape[-1] // 2
    hist = CONF_KW - 1
    rc = 64
    assert t_len % tm == 0 and tm % rc == 0 and t_len >= hist
    widths = (d, wqkv.shape[-1], wz.shape[-1], wba.shape[-1], wm.shape[-1])

    def wspec(w):
        return _const_spec((None,) + w.shape[1:], lambda g, i: (layer, 0, 0))

    return pl.pallas_call(
        functools.partial(_inproj_conf_kernel, conf_ch=conf_ch, tm=tm, rc=rc),
        grid=(bsz, t_len // tm),
        in_specs=[
            pl.BlockSpec((None, tm, d), lambda g, i: (g, i, 0)),
            _mod_spec(mod, tm, 1),
            _mod_spec(mod, tm, 0),
            wspec(norm_g), wspec(wglu), wspec(wqkv), wspec(wz), wspec(wba), wspec(wm),
            wspec(dw_w), wspec(dw_b), wspec(ln_g), wspec(ln_b), wspec(wco),
        ],
        out_specs=[pl.BlockSpec((None, tm, w), lambda g, i: (g, i, 0)) for w in widths]
        + [pl.BlockSpec((None, hist, conf_ch), lambda g, i: (g, 0, 0))],
        out_shape=[jax.ShapeDtypeStruct((bsz, t_len, w), F32) for w in widths]
        + [jax.ShapeDtypeStruct((bsz, hist, conf_ch), F32)],
        scratch_shapes=[pltpu.VMEM((tm + CONF_TAIL, conf_ch), F32), pltpu.VMEM((tm, conf_ch), F32)],
        compiler_params=_params(2),
        name="in_proj_conformer",
    )(x, mod, mod, norm_g, wglu, wqkv, wz, wba, wm, dw_w, dw_b, ln_g, ln_b, wco)


def _conf_step_kernel(u_ref, st_ref, w_ref, b_ref, lng_ref, lnb_ref, wout_ref, ya_ref, nst_ref, ca_ref):
    bb = u_ref.shape[0]
    hist = CONF_KW - 1
    w_hist = w_ref[0:hist, :]
    w_new = w_ref[hist:hist + 1, :]
    for b in range(bb):
        u_row = u_ref[b:b + 1, :]
        ca_ref[b:b + 1, :] = jnp.sum(st_ref[b] * w_hist, axis=0, keepdims=True) + u_row * w_new
        nst_ref[b, 0:hist - 1, :] = st_ref[b, 1:hist, :]
        nst_ref[b, hist - 1:hist, :] = u_row
    act = _layer_norm_swish(ca_ref[...] + b_ref[...], lng_ref[...], lnb_ref[...])
    ya_ref[...] = _bdot(act, wout_ref[...])


def _conf_step_call(u, st, dw_w, dw_b, ln_g, ln_b, wout, layer, bb):
    bsz, ch = u.shape
    d = wout.shape[-1]
    hist = st.shape[2]

    def wspec(w):
        return _const_spec((None,) + w.shape[1:], lambda i: (layer, 0, 0))

    return pl.pallas_call(
        _conf_step_kernel,
        grid=(bsz // bb,),
        in_specs=[pl.BlockSpec((bb, ch), lambda i: (i, 0)),
                  pl.BlockSpec((None, bb, hist, ch), lambda i: (layer, i, 0, 0)),
                  wspec(dw_w), wspec(dw_b), wspec(ln_g), wspec(ln_b), wspec(wout)],
        out_specs=[pl.BlockSpec((bb, d), lambda i: (i, 0)),
                   pl.BlockSpec((bb, hist, ch), lambda i: (i, 0, 0))],
        out_shape=[jax.ShapeDtypeStruct((bsz, d), F32),
                   jax.ShapeDtypeStruct((bsz, hist, ch), F32)],
        scratch_shapes=[pltpu.VMEM((bb, ch), F32)],
        compiler_params=_params(1),
        name="conformer_step",
    )(u, st, dw_w, dw_b, ln_g, ln_b, wout)


def _head_column(x, lane):
    ids = lax.broadcasted_iota(jnp.int32, x.shape, 1)
    return jnp.sum(jnp.where(ids == lane, x, 0.0), axis=-1, keepdims=True)


def _expand_heads(x, first_lane):
    rows = x.shape[0]
    return jnp.concatenate(
        [jnp.broadcast_to(_head_column(x, first_lane + h), (rows, HEAD_DIM)) for h in range(N_HEADS)], axis=1)


def _l2_normalize_heads(x):
    outs = []
    for h in range(N_HEADS):
        xh = x[:, h * HEAD_DIM:(h + 1) * HEAD_DIM]
        outs.append(xh * lax.rsqrt(jnp.sum(xh * xh, axis=-1, keepdims=True) + L2_EPS))
    return jnp.concatenate(outs, axis=1)


def _gated_out_norm(o, z, ng):
    outs = []
    for h in range(N_HEADS):
        oh = o[:, h * HEAD_DIM:(h + 1) * HEAD_DIM]
        outs.append(oh * lax.rsqrt(jnp.mean(oh * oh, axis=-1, keepdims=True) + RMS_EPS))
    return jnp.concatenate(outs, axis=1) * ng * _silu(z)


def _beta_and_log_decay(ba, alog, dtb):
    beta = jax.nn.sigmoid(ba)
    g = -jnp.exp(alog) * _softplus(ba + dtb)
    return beta, g


def _delta_seq_kernel(qkv_ref, z_ref, ba_ref, scw_ref, alog_ref, dtb_ref, ng_ref, wout_ref,
                      yb_ref, sbuf_ref, state_ref,
                      ext_ref, o_ref, q_ref, k_ref, kb_ref, rhs_ref, qg_ref, gx_ref,
                      lmat_ref, qk_ref, lt_ref, nt_ref, nmat_ref, *, tm):
    t = pl.program_id(1)
    n_seq = qkv_ref.shape[0]
    hist = SHORT_KW - 1
    qk_dim = N_HEADS * HEAD_DIM
    cl = DELTA_CHUNK
    seq_chunks = tm // cl
    n_chunks = n_seq * seq_chunks
    n_sys = n_chunks * N_HEADS

    @pl.when(t == 0)
    def _():
        for s in range(n_seq):
            ext_ref[s, 0:SHORT_TAIL, :] = jnp.zeros((SHORT_TAIL, ext_ref.shape[2]), F32)
        state_ref[...] = jnp.zeros(state_ref.shape, F32)

    @pl.when(t > 0)
    def _():
        for s in range(n_seq):
            ext_ref[s, 0:SHORT_TAIL, :] = ext_ref[s, tm:tm + SHORT_TAIL, :]

    convs = []
    for s in range(n_seq):
        ext_ref[s, SHORT_TAIL:SHORT_TAIL + tm, :] = qkv_ref[s]
        conv = ext_ref[s, SHORT_TAIL - hist:SHORT_TAIL - hist + tm, :] * scw_ref[0:1, :]
        for j in range(1, SHORT_KW):
            conv = conv + ext_ref[s, SHORT_TAIL - hist + j:SHORT_TAIL - hist + j + tm, :] * scw_ref[j:j + 1, :]
        convs.append(conv)
    act = _silu(jnp.concatenate(convs, axis=0))
    q = _l2_normalize_heads(act[:, 0:qk_dim]) * (HEAD_DIM ** -0.5)
    k = _l2_normalize_heads(act[:, qk_dim:2 * qk_dim])
    v = act[:, 2 * qk_dim:]

    ba = jnp.concatenate([ba_ref[s] for s in range(n_seq)], axis=0)
    beta, g = _beta_and_log_decay(ba, alog_ref[...], dtb_ref[...])
    pos = lax.broadcasted_iota(jnp.int32, g.shape, 0) % cl
    shift = 1
    while shift < cl:
        g = g + jnp.where(pos >= shift, pltpu.roll(g, shift, 0), 0.0)
        shift *= 2
    beta_x = _expand_heads(beta, 0)
    g_x = _expand_heads(g, N_HEADS)
    eg_x = jnp.exp(g_x)
    kb = k * beta_x
    q_ref[...] = q
    k_ref[...] = k
    kb_ref[...] = kb
    qg_ref[...] = q * eg_x
    gx_ref[...] = g_x
    rhs_u = v * beta_x
    rhs_w = kb * eg_x
    for h in range(N_HEADS):
        hs = slice(h * HEAD_DIM, (h + 1) * HEAD_DIM)
        rhs_ref[:, 2 * h * HEAD_DIM:(2 * h + 1) * HEAD_DIM] = rhs_u[:, hs]
        rhs_ref[:, (2 * h + 1) * HEAD_DIM:(2 * h + 2) * HEAD_DIM] = rhs_w[:, hs]

    row = lax.broadcasted_iota(jnp.int32, (cl, cl), 0)
    col = lax.broadcasted_iota(jnp.int32, (cl, cl), 1)
    causal = row >= col
    strict = row > col

    def build(c, carry):
        r0 = pl.multiple_of(c * cl, cl)
        for h in range(N_HEADS):
            hs = slice(h * HEAD_DIM, (h + 1) * HEAD_DIM)
            g_i = gx_ref[pl.ds(r0, cl), hs]
            g_j = g_i.T[0:cl, :]
            decay = jnp.where(causal, jnp.exp(jnp.where(causal, g_i[:, 0:cl] - g_j, 0.0)), 0.0)
            k_c = k_ref[pl.ds(r0, cl), hs]
            kk = _bdot_nt(kb_ref[pl.ds(r0, cl), hs], k_c)
            s0 = pl.multiple_of((c * N_HEADS + h) * cl, cl)
            lmat_ref[pl.ds(s0, cl), :] = jnp.where(strict, kk * decay, 0.0)
            qk_ref[pl.ds(s0, cl), :] = _bdot_nt(q_ref[pl.ds(r0, cl), hs], k_c) * decay
        return carry

    lax.fori_loop(0, n_chunks, build, 0, unroll=2)

    for i in range(1, cl):
        lt_ref[i] = lmat_ref[pl.ds(i, n_sys, stride=cl), :].T
    nt_ref[...] = jnp.zeros(nt_ref.shape, F32)
    sub = 8
    for i in range(1, cl):
        acc = [-lt_ref[i, sub * r:sub * (r + 1), :] for r in range((i - 1) // sub + 1)]
        for j in range(1, i):
            l_ij = lt_ref[i, j:j + 1, :]
            for r in range((j - 1) // sub + 1):
                acc[r] = acc[r] - l_ij * nt_ref[j, sub * r:sub * (r + 1), :]
        for r, a in enumerate(acc):
            nt_ref[i, sub * r:sub * (r + 1), :] = a
    for i in range(cl):
        nmat_ref[pl.ds(i, n_sys, stride=cl), :] = nt_ref[i].T

    def solve(c, carry):
        r0 = pl.multiple_of(c * cl, cl)
        for h in range(N_HEADS):
            s0 = pl.multiple_of((c * N_HEADS + h) * cl, cl)
            cols = slice(2 * h * HEAD_DIM, (2 * h + 2) * HEAD_DIM)
            rhs = rhs_ref[pl.ds(r0, cl), cols]
            n_s = nmat_ref[pl.ds(s0, cl), :]
            n_lo = n_s - n_s.astype(BF16).astype(F32)
            r_lo = rhs - rhs.astype(BF16).astype(F32)
            rhs_ref[pl.ds(r0, cl), cols] = rhs + _bdot(jnp.concatenate([n_s, n_lo, n_s], axis=1),
                                                       jnp.concatenate([rhs, rhs, r_lo], axis=0))
        return carry

    lax.fori_loop(0, n_chunks, solve, 0, unroll=2)

    def step(j, carry):
        for s in range(n_seq):
            c = s * seq_chunks + j
            r0 = pl.multiple_of(c * cl, cl)
            for h in range(N_HEADS):
                hs = slice(h * HEAD_DIM, (h + 1) * HEAD_DIM)
                s0 = pl.multiple_of((c * N_HEADS + h) * cl, cl)
                s_old = state_ref[s, h]
                ws_qs = _bdot(jnp.concatenate(
                    [rhs_ref[pl.ds(r0, cl), (2 * h + 1) * HEAD_DIM:(2 * h + 2) * HEAD_DIM],
                     qg_ref[pl.ds(r0, cl), hs]], axis=0), s_old)
                v_new = rhs_ref[pl.ds(r0, cl), 2 * h * HEAD_DIM:(2 * h + 1) * HEAD_DIM] - ws_qs[0:cl, :]
                g_i = gx_ref[pl.ds(r0, cl), hs]
                g_last = g_i[cl - 1:cl, :]
                k_dec = k_ref[pl.ds(r0, cl), hs] * jnp.exp(g_last - g_i)
                ov_sv = _bdot(jnp.concatenate([qk_ref[pl.ds(s0, cl), :], k_dec.T], axis=0), v_new)
                o_ref[pl.ds(r0, cl), hs] = ws_qs[cl:2 * cl, :] + ov_sv[0:cl, :]
                state_ref[s, h] = s_old * jnp.exp(g_last) + ov_sv[cl:, :]
        return carry

    lax.fori_loop(0, seq_chunks, step, 0, unroll=2)

    z = jnp.concatenate([z_ref[s] for s in range(n_seq)], axis=0)
    y = _bdot(_gated_out_norm(o_ref[...], z, ng_ref[...]), wout_ref[...])
    for s in range(n_seq):
        yb_ref[s] = y[s * tm:(s + 1) * tm, :]

    @pl.when(t == pl.num_programs(1) - 1)
    def _():
        for s in range(n_seq):
            sbuf_ref[s] = ext_ref[s, tm + SHORT_TAIL - hist:tm + SHORT_TAIL, :]


def _delta_seq_call(qkv, z, ba, scw, alog, dtb, ng, wout, layer, tm, n_seq):
    bsz, t_len, qkv_dim = qkv.shape
    v_dim = z.shape[-1]
    d = wout.shape[-1]
    hist = SHORT_KW - 1
    assert t_len % tm == 0 and tm % DELTA_CHUNK == 0 and bsz % n_seq == 0
    cl = DELTA_CHUNK
    rows = n_seq * tm
    n_sys = (rows // cl) * N_HEADS

    def wspec(w):
        return _const_spec((None,) + w.shape[1:], lambda b, t: (layer, 0, 0))

    return pl.pallas_call(
        functools.partial(_delta_seq_kernel, tm=tm),
        grid=(bsz // n_seq, t_len // tm),
        in_specs=[pl.BlockSpec((n_seq, tm, qkv_dim), lambda b, t: (b, t, 0)),
                  pl.BlockSpec((n_seq, tm, v_dim), lambda b, t: (b, t, 0)),
                  pl.BlockSpec((n_seq, tm, LANES), lambda b, t: (b, t, 0)),
                  wspec(scw), wspec(alog), wspec(dtb), wspec(ng), wspec(wout)],
        out_specs=[pl.BlockSpec((n_seq, tm, d), lambda b, t: (b, t, 0)),
                   pl.BlockSpec((n_seq, hist, qkv_dim), lambda b, t: (b, 0, 0)),
                   pl.BlockSpec((n_seq, N_HEADS, HEAD_DIM, HEAD_DIM), lambda b, t: (b, 0, 0, 0))],
        out_shape=[jax.ShapeDtypeStruct((bsz, t_len, d), F32),
                   jax.ShapeDtypeStruct((bsz, hist, qkv_dim), F32),
                   jax.ShapeDtypeStruct((bsz, N_HEADS, HEAD_DIM, HEAD_DIM), F32)],
        scratch_shapes=[pltpu.VMEM((n_seq, tm + SHORT_TAIL, qkv_dim), F32),
                        pltpu.VMEM((rows, v_dim), F32),
                        pltpu.VMEM((rows, v_dim), F32),
                        pltpu.VMEM((rows, v_dim), F32),
                        pltpu.VMEM((rows, v_dim), F32),
                        pltpu.VMEM((rows, 2 * v_dim), F32),
                        pltpu.VMEM((rows, v_dim), F32),
                        pltpu.VMEM((rows, v_dim), F32),
                        pltpu.VMEM((n_sys * cl, cl), F32),
                        pltpu.VMEM((n_sys * cl, cl), F32),
                        pltpu.VMEM((cl, cl, n_sys), F32),
                        pltpu.VMEM((cl, cl, n_sys), F32),
                        pltpu.VMEM((n_sys * cl, cl), F32)],
        compiler_params=_params(2),
        name="delta_seq",
    )(qkv, z, ba, scw, alog, dtb, ng, wout)


def _delta_step_kernel(qkv_ref, z_ref, ba_ref, sst_ref, ds_ref, scw_ref, alog_ref, dtb_ref, ng_ref, wout_ref,
                       yb_ref, nsst_ref, nds_ref, o_ref, conv_ref):
    bb = qkv_ref.shape[0]
    hist = SHORT_KW - 1
    qk_dim = N_HEADS * HEAD_DIM
    w_hist = scw_ref[0:hist, :]
    w_new = scw_ref[hist:hist + 1, :]
    for b in range(bb):
        x_row = qkv_ref[b:b + 1, :]
        conv_ref[b:b + 1, :] = jnp.sum(sst_ref[b] * w_hist, axis=0, keepdims=True) + x_row * w_new
        nsst_ref[b, 0:hist - 1, :] = sst_ref[b, 1:hist, :]
        nsst_ref[b, hist - 1:hist, :] = x_row
    act = _silu(conv_ref[...])
    q = _l2_normalize_heads(act[:, 0:qk_dim]) * (HEAD_DIM ** -0.5)
    k = _l2_normalize_heads(act[:, qk_dim:2 * qk_dim])
    v = act[:, 2 * qk_dim:]
    beta, g = _beta_and_log_decay(ba_ref[...], alog_ref[...], dtb_ref[...])
    beta_x = _expand_heads(beta, 0)
    eg_x = jnp.exp(_expand_heads(g, N_HEADS))
    kb = k * beta_x
    u_all = v * beta_x
    w_all = kb * eg_x
    qg = q * eg_x
    sub = lax.broadcasted_iota(jnp.int32, (8, HEAD_DIM), 0)
    for b in range(bb):
        for h in range(N_HEADS):
            hs = slice(h * HEAD_DIM, (h + 1) * HEAD_DIM)
            s_old = ds_ref[b, h]
            lhs = jnp.where(sub == 0, w_all[b:b + 1, hs], jnp.where(sub == 1, qg[b:b + 1, hs], 0.0))
            prod = _bdot(lhs, s_old)
            v_new = u_all[b:b + 1, hs] - prod[0:1, :]
            k_row = k[b:b + 1, hs]
            qk = jnp.sum(q[b:b + 1, hs] * k_row, axis=-1, keepdims=True)
            o_ref[b:b + 1, hs] = prod[1:2, :] + qk * v_new
            k_col = jnp.broadcast_to(k_row, (HEAD_DIM, HEAD_DIM)).T
            nds_ref[b, h] = s_old * eg_x[b:b + 1, hs] + k_col * v_new
    yb_ref[...] = _bdot(_gated_out_norm(o_ref[...], z_ref[...], ng_ref[...]), wout_ref[...])


def _delta_step_call(qkv, z, ba, sst, ds, scw, alog, dtb, ng, wout, layer, bb):
    bsz, qkv_dim = qkv.shape
    v_dim = z.shape[-1]
    d = wout.shape[-1]
    hist = sst.shape[2]

    def wspec(w):
        return _const_spec((None,) + w.shape[1:], lambda i: (layer, 0, 0))

    state_block = (bb, N_HEADS, HEAD_DIM, HEAD_DIM)
    return pl.pallas_call(
        _delta_step_kernel,
        grid=(bsz // bb,),
        in_specs=[pl.BlockSpec((bb, qkv_dim), lambda i: (i, 0)),
                  pl.BlockSpec((bb, v_dim), lambda i: (i, 0)),
                  pl.BlockSpec((bb, LANES), lambda i: (i, 0)),
                  pl.BlockSpec((None, bb, hist, qkv_dim), lambda i: (layer, i, 0, 0)),
                  pl.BlockSpec((None,) + state_block, lambda i: (layer, i, 0, 0, 0)),
                  wspec(scw), wspec(alog), wspec(dtb), wspec(ng), wspec(wout)],
        out_specs=[pl.BlockSpec((bb, d), lambda i: (i, 0)),
                   pl.BlockSpec((bb, hist, qkv_dim), lambda i: (i, 0, 0)),
                   pl.BlockSpec(state_block, lambda i: (i, 0, 0, 0))],
        out_shape=[jax.ShapeDtypeStruct((bsz, d), F32),
                   jax.ShapeDtypeStruct((bsz, hist, qkv_dim), F32),
                   jax.ShapeDtypeStruct((bsz, N_HEADS, HEAD_DIM, HEAD_DIM), F32)],
        scratch_shapes=[pltpu.VMEM((bb, v_dim), F32), pltpu.VMEM((bb, qkv_dim), F32)],
        compiler_params=_params(1),
        name="delta_step",
    )(qkv, z, ba, sst, ds, scw, alog, dtb, ng, wout)


def _merge_ffn_kernel(x_ref, ya_ref, yb_ref, gate_ref, gt1_ref, sc2_ref, sh2_ref, gt2_ref, n2g_ref,
                      wm_ref, wfi_ref, wfo_ref, fin_ref, out_ref, *, ff_chunk, final_norm):
    d = x_ref.shape[-1]
    d_ff = wfo_ref.shape[0]
    merged = gate_ref[:, 0:d] * ya_ref[...] + gate_ref[:, d:2 * d] * yb_ref[...]
    x1 = x_ref[...] + gt1_ref[...] * _bdot(merged, wm_ref[...])
    h2 = x1 * lax.rsqrt(jnp.mean(x1 * x1, axis=-1, keepdims=True) + RMS_EPS) * n2g_ref[...]
    hb = (h2 * (1.0 + sc2_ref[...]) + sh2_ref[...]).astype(BF16)
    acc = jnp.zeros(x1.shape, F32)
    for c0 in range(0, d_ff, ff_chunk):
        gate = jnp.dot(hb, wfi_ref[:, c0:c0 + ff_chunk], preferred_element_type=F32)
        up = jnp.dot(hb, wfi_ref[:, d_ff + c0:d_ff + c0 + ff_chunk], preferred_element_type=F32)
        acc = acc + _bdot(_silu(gate) * up, wfo_ref[c0:c0 + ff_chunk, :])
    x2 = x1 + gt2_ref[...] * acc
    if final_norm:
        x2 = x2 * lax.rsqrt(jnp.mean(x2 * x2, axis=-1, keepdims=True) + RMS_EPS) * fin_ref[...]
    out_ref[...] = x2


def _merge_ffn_call(x, ya, yb, gates, mod, n2g, wm, wfi, wfo, fin_g, layer, tm, final_norm):
    groups, rows, d = x.shape
    d_ff = wfo.shape[1]
    ff_chunk = 256
    assert d_ff % ff_chunk == 0

    def wspec(w):
        return _const_spec((None,) + w.shape[1:], lambda g, i: (layer, 0, 0))

    def rowspec(w):
        return pl.BlockSpec((None, tm, w), lambda g, i: (g, i, 0))

    return pl.pallas_call(
        functools.partial(_merge_ffn_kernel, ff_chunk=ff_chunk, final_norm=final_norm),
        grid=(groups, rows // tm),
        in_specs=[rowspec(d), rowspec(d), rowspec(d), rowspec(2 * d),
                  _mod_spec(mod, tm, 2), _mod_spec(mod, tm, 4), _mod_spec(mod, tm, 3), _mod_spec(mod, tm, 5),
                  wspec(n2g), wspec(wm), wspec(wfi), wspec(wfo),
                  _const_spec((1, d), lambda g, i: (0, 0))],
        out_specs=rowspec(d),
        out_shape=jax.ShapeDtypeStruct((groups, rows, d), F32),
        compiler_params=_params(2),
        name="merge_ffn",
    )(x, ya, yb, gates, mod, mod, mod, mod, n2g, wm, wfi, wfo, fin_g)


def kernel(x_prompt, x_sample, c_prompt, c_sample, state_conformer_conv, state_short_conv, state_delta, w_ada, b_ada, norm1_g, w_in, conf_dw_w, conf_dw_b, conf_ln_g, conf_ln_b, w_conf_out, short_conv_w, a_log, dt_bias, delta_norm_g, w_delta_out, w_merge_out, norm2_g, w_ffn_in, w_ffn_out, final_norm_g):
    depth, d, in_dim = w_in.shape
    bp, t_len, _ = x_prompt.shape
    bs = x_sample.shape[0]
    assert x_sample.shape[1] == 1
    conf_ch = conf_dw_w.shape[-1]
    qkv_dim = short_conv_w.shape[-1]
    v_dim = w_delta_out.shape[1]
    assert v_dim == N_HEADS * HEAD_DIM and qkv_dim == 3 * v_dim
    assert conf_dw_w.shape[1] == CONF_KW and short_conv_w.shape[1] == SHORT_KW

    o_glu = 2 * conf_ch
    o_qkv = o_glu + qkv_dim
    o_z = o_qkv + v_dim
    o_ba = o_z + 2 * N_HEADS
    assert in_dim == o_ba + 2 * d
    wglu = w_in[:, :, :o_glu].astype(BF16)
    wqkv = w_in[:, :, o_glu:o_qkv].astype(BF16)
    wz = w_in[:, :, o_qkv:o_z].astype(BF16)
    wba = jnp.pad(w_in[:, :, o_z:o_ba].astype(BF16), ((0, 0), (0, 0), (0, LANES - 2 * N_HEADS)))
    wmg = w_in[:, :, o_ba:].astype(BF16)
    w_conf_out_b = w_conf_out.astype(BF16)
    w_delta_out_b = w_delta_out.astype(BF16)
    w_merge_b = w_merge_out.astype(BF16)
    w_ffn_in_b = w_ffn_in.astype(BF16)
    w_ffn_out_b = w_ffn_out.astype(BF16)

    def row3(a):
        return a.reshape(depth, 1, a.shape[-1])

    norm1 = row3(norm1_g)
    norm2 = row3(norm2_g)
    dw_b = row3(conf_dw_b)
    ln_g = row3(conf_ln_g)
    ln_b = row3(conf_ln_b)
    alog = jnp.pad(a_log, ((0, 0), (N_HEADS, LANES - 2 * N_HEADS))).reshape(depth, 1, LANES)
    dtb = jnp.pad(dt_bias, ((0, 0), (N_HEADS, LANES - 2 * N_HEADS))).reshape(depth, 1, LANES)
    ng = row3(jnp.tile(delta_norm_g, (1, N_HEADS)))
    fin_g = final_norm_g.reshape(1, d)

    mod = _ada_call(jnp.concatenate([c_prompt, c_sample], axis=0), w_ada, b_ada)
    mod_p = mod[:, :bp].reshape(depth, bp, 1, 6 * d)
    mod_s = mod[:, bp:].reshape(depth, 1, bs, 6 * d)
    xp = x_prompt
    xs = x_sample.reshape(1, bs, d)
    conf_p, conf_s, short_p, short_s, delta_p, delta_s = [], [], [], [], [], []
    for l in range(depth):
        last = l == depth - 1
        ya, qkv, z, ba, gates, cb = _inproj_conf_call(xp, mod_p[l], norm1, wglu, wqkv, wz, wba, wmg,
                                                      conf_dw_w, dw_b, ln_g, ln_b, w_conf_out_b, l, tm=512)
        yb, sb, ds = _delta_seq_call(qkv, z, ba, short_conv_w, alog, dtb, ng, w_delta_out_b, l, tm=256, n_seq=2)
        xp = _merge_ffn_call(xp, ya, yb, gates, mod_p[l], norm2, w_merge_b, w_ffn_in_b, w_ffn_out_b, fin_g,
                             l, tm=512, final_norm=last)
        conf_p.append(cb)
        short_p.append(sb)
        delta_p.append(ds)
        u, qkv, z, ba, gates = _inproj_call(xs, mod_s[l], norm1, wglu, wqkv, wz, wba, wmg, l, tm=bs)
        ya, cb = _conf_step_call(u[0], state_conformer_conv, conf_dw_w, dw_b, ln_g, ln_b, w_conf_out_b, l, bb=32)
        yb, sb, ds = _delta_step_call(qkv[0], z[0], ba[0], state_short_conv, state_delta, short_conv_w, alog, dtb,
                                      ng, w_delta_out_b, l, bb=8)
        xs = _merge_ffn_call(xs, ya[None], yb[None], gates, mod_s[l], norm2, w_merge_b, w_ffn_in_b, w_ffn_out_b,
                             fin_g, l, tm=bs, final_norm=last)
        conf_s.append(cb)
        short_s.append(sb)
        delta_s.append(ds)
    return (xp, xs.reshape(bs, 1, d), jnp.stack(conf_p), jnp.stack(conf_s), jnp.stack(short_p),
            jnp.stack(short_s), jnp.stack(delta_p), jnp.stack(delta_s))
```
